```python
import math
import jax
import jax.numpy as jnp
from jax import lax
import numpy as np

D_MODEL = 2048
BATCH = 32
SEQ = 256
DEPTH = 2
DEC_BATCH = 4
DEC_SEQ = 4096
PAST_LEN = 512

GRID_W = 64
N_EVEN = (DEPTH + 1) // 2
N_ODD = DEPTH // 2
EPS = 1e-6

D_HY = D_MODEL // 2
HY_EMB = 33
HY_FH = 64
HY_TARGET = 1e-2
HY_FAST = 0.3
HY_SLOW = 1.5

D_RW = D_MODEL // 2
RW_HEAD = 64
RW_H = D_RW // RW_HEAD
RW_W_LORA = 64
RW_A_LORA = 64
RW_G_LORA = 128
RW_GN_EPS = 64e-5
RW_IN = 3 * D_RW + 2 * RW_W_LORA + 2 * RW_A_LORA + RW_G_LORA
EV_IN = 3 * D_HY + RW_IN

GLA_H = 4
GLA_DK = D_MODEL // 2
GLA_DV = D_MODEL
GLA_HK = GLA_DK // GLA_H
GLA_HV = GLA_DV // GLA_H
GLA_GK_RANK = 16
GLA_GATE_NORM = 16.0
GLA_CHUNK = 64
OD_IN = 2 * GLA_DK + 2 * GLA_DV + 2 * GLA_GK_RANK

D_FF = 5632

kernel_name = 'hybrid_hyena_rwkv7_gla_diffusion_step'


def rms_norm(x, g, eps=EPS):
    xf = x.astype(jnp.float32)
    y = xf * lax.rsqrt(jnp.mean(jnp.square(xf), axis=-1, keepdims=True) + eps)
    return (y * g.astype(jnp.float32)).astype(x.dtype)


def adaln(cond, w, b):
    mod = (jax.nn.silu(cond) @ w + b)[:, None, :]
    return jnp.split(mod, 6, axis=-1)


def modulate(h, shift, scale):
    return h * (1 + scale) + shift


def neighbour_mean(x):
    xp = jnp.pad(x, ((0, 0), (1, 1), (0, 0)))
    return 0.5 * (xp[:, :-2] + xp[:, 2:])


def conv1d_centred(x, w, b):
    xp = jnp.pad(x, ((0, 0), (1, 1), (0, 0)))
    return xp[:, :-2] * w[0] + xp[:, 1:-1] * w[1] + xp[:, 2:] * w[2] + b


def dwconv_grid(x, w, rows, cols):
    B, L, C = x.shape
    gp = jnp.pad(x.reshape(B, rows, cols, C), ((0, 0), (1, 1), (1, 1), (0, 0)))
    y = sum(gp[:, i:i + rows, j:j + cols] * w[i, j] for i in range(3) for j in range(3))
    return y.reshape(B, L, C)


def hyena_filter(L, w1, b1, w2, b2, w3, freq):
    f32 = jnp.float32
    pos = jnp.arange(L, dtype=f32)[:, None]
    t = pos / max(L - 1, 1)
    bands = (HY_EMB - 1) // 2
    fb = jnp.linspace(1e-4, bands - 1, bands, dtype=f32)
    ang = (2.0 * math.pi / L) * pos * fb
    z = jnp.concatenate([t, jnp.cos(ang), -jnp.sin(ang)], axis=-1)
    h = jnp.sin(freq[0].astype(f32) * (z @ w1.astype(f32) + b1.astype(f32)))
    h = jnp.sin(freq[1].astype(f32) * (h @ w2.astype(f32) + b2.astype(f32)))
    h = (h @ w3.astype(f32)).reshape(L, 2, D_HY)
    deltas = jnp.abs(jnp.linspace(math.log(HY_TARGET) / HY_SLOW, math.log(HY_TARGET) / HY_FAST, D_HY, dtype=f32))
    h = h * jnp.exp(-t[:, :, None] * deltas)
    filt = jnp.concatenate([h[:, 0], jnp.zeros((1, D_HY), f32), jnp.flip(h[1:, 1], axis=0)], axis=0)
    return filt / jnp.sum(jnp.abs(filt), axis=0, keepdims=True)


def fft_long_conv(u, filt, bias):
    L = u.shape[1]
    uf = u.astype(jnp.float32)
    y = jnp.fft.irfft(jnp.fft.rfft(uf, n=2 * L, axis=1) * jnp.fft.rfft(filt, n=2 * L, axis=0)[None], n=2 * L, axis=1)[:, :L]
    return (y + uf * bias.astype(jnp.float32)).astype(u.dtype)


def hyena_mixer(P, short_w, short_b, w1, b1, w2, b2, w3, freq, bias):
    L = P.shape[1]
    x0, x1, v = jnp.split(conv1d_centred(P, short_w, short_b), 3, axis=-1)
    filt = hyena_filter(L, w1, b1, w2, b2, w3, freq)
    return x0 * fft_long_conv(x1 * v, filt, bias)


def rwkv7_scan(S0, r, w, k, v, kk, a):
    def step(S, inp):
        r_t, w_t, k_t, v_t, kk_t, a_t = inp
        sa = jnp.einsum('bhvk,bhk->bhv', S, -kk_t)
        S = S * w_t[:, :, None, :] + sa[..., None] * (kk_t * a_t)[:, :, None, :] + v_t[..., None] * k_t[:, :, None, :]
        return S, jnp.einsum('bhvk,bhk->bhv', S, r_t)
    xs = tuple(jnp.moveaxis(t, 1, 0) for t in (r, w, k, v, kk, a))
    S, ys = lax.scan(step, S0.astype(jnp.float32), xs)
    return S, jnp.moveaxis(ys, 0, 1)


def rwkv7_mixer(P, S0, mu, w0, w2, a0, a2, g2, k_k, k_a, r_k, ln_w, ln_b):
    f32 = jnp.float32
    B, L, _ = P.shape
    P = P + mu * (neighbour_mean(P) - P)
    r, k, v, wd, ad, gd = jnp.split(P, [D_RW, 2 * D_RW, 3 * D_RW, 3 * D_RW + 2 * RW_W_LORA, 3 * D_RW + 2 * RW_W_LORA + 2 * RW_A_LORA], axis=-1)
    wd = wd.reshape(B, L, 2, RW_W_LORA)
    ad = ad.reshape(B, L, 2, RW_A_LORA)
    w_log = -jax.nn.softplus(-(w0 + jnp.einsum('bldr,drc->bldc', jnp.tanh(wd), w2)).astype(f32)) - 0.5
    decay = jnp.exp(-jnp.exp(w_log))
    a = jax.nn.sigmoid((a0 + jnp.einsum('bldr,drc->bldc', ad, a2)).astype(f32))
    g = jax.nn.sigmoid(gd) @ g2

    def heads(t):
        return t.astype(f32).reshape(t.shape[:-1] + (RW_H, RW_HEAD))

    r_h, v_h = heads(r), heads(v)
    kk = heads(k * k_k)
    kk = kk * lax.rsqrt(jnp.sum(kk * kk, axis=-1, keepdims=True) + 1e-12)
    k_dir = heads(k[:, :, None, :] * (1 + (a - 1) * k_a))
    decay_h, a_h = heads(decay), heads(a)

    def flip(t):
        return jnp.flip(t, axis=1)

    S_f, y_f = rwkv7_scan(S0[:, 0], r_h, decay_h[:, :, 0], k_dir[:, :, 0], v_h, kk, a_h[:, :, 0])
    S_b, y_b = rwkv7_scan(S0[:, 1], flip(r_h), flip(decay_h[:, :, 1]), flip(k_dir[:, :, 1]), flip(v_h), flip(kk), flip(a_h[:, :, 1]))
    y = y_f + flip(y_b)
    mean = jnp.mean(y, axis=-1, keepdims=True)
    var = jnp.var(y, axis=-1, keepdims=True)
    y = ((y - mean) * lax.rsqrt(var + RW_GN_EPS)).reshape(B, L, D_RW) * ln_w + ln_b
    bonus = jnp.einsum('blhn,bldhn,hn->blh', r_h, k_dir, r_k.astype(f32))[..., None] * v_h
    out = (y + bonus.reshape(B, L, D_RW)) * g
    return out.astype(P.dtype), jnp.stack([S_f, S_b], axis=1).astype(P.dtype)


def gla_chunked(q, k, v, lg, S0):
    f32 = jnp.float32
    B, L, H, K = q.shape
    V = v.shape[-1]
    C = GLA_CHUNK
    n = L // C

    def chunks(t):
        return t.astype(f32).reshape(B, n, C, H, t.shape[-1])

    qc, kc, vc, lgc = chunks(q), chunks(k), chunks(v), chunks(lg)
    b = jnp.cumsum(lgc, axis=2)
    b_ref = b[:, :, C // 2 - 1:C // 2]
    scores = jnp.einsum('bnthk,bnshk->bnhts', qc * jnp.exp(b - b_ref), kc * jnp.exp(b_ref - b))
    causal = jnp.tril(jnp.ones((C, C), dtype=bool))
    scores = jnp.where(causal, scores, 0.0)
    o_intra = jnp.einsum('bnhts,bnshv->bnthv', scores, vc)
    b_last = b[:, :, -1:]
    q_dec = qc * jnp.exp(b)
    k_dec = kc * jnp.exp(b_last - b)
    chunk_decay = jnp.exp(b_last[:, :, 0])

    def step(S, inp):
        q_t, k_t, v_t, d_t = inp
        o = jnp.einsum('bthk,bhkv->bthv', q_t, S)
        S = S * d_t[..., None] + jnp.einsum('bthk,bthv->bhkv', k_t, v_t)
        return S, o

    xs = tuple(jnp.moveaxis(t, 1, 0) for t in (q_dec, k_dec, vc, chunk_decay))
    S, o_inter = lax.scan(step, S0.astype(f32), xs)
    o = o_intra + jnp.moveaxis(o_inter, 0, 1)
    return S, o.reshape(B, L, H, V)


def gla_mixer(P, S0, gk2, gk_b, norm_w):
    f32 = jnp.float32
    B, L, _ = P.shape
    q, k, v, g, gkd = jnp.split(P, [GLA_DK, 2 * GLA_DK, 2 * GLA_DK + GLA_DV, 2 * GLA_DK + 2 * GLA_DV], axis=-1)
    gkd = gkd.reshape(B, L, 2, GLA_GK_RANK)
    lg = jax.nn.log_sigmoid((jnp.einsum('bldr,drk->bldk', gkd, gk2) + gk_b).astype(f32)) / GLA_GATE_NORM
    qh = q.reshape(B, L, GLA_H, GLA_HK) * (GLA_HK ** -0.5)
    kh = k.reshape(B, L, GLA_H, GLA_HK)
    vh = v.reshape(B, L, GLA_H, GLA_HV)
    lgh = lg.reshape(B, L, 2, GLA_H, GLA_HK)

    def flip(t):
        return jnp.flip(t, axis=1)

    S_f, o_f = gla_chunked(qh, kh, vh, lgh[:, :, 0], S0[:, 0])
    S_b, o_b = gla_chunked(flip(qh), flip(kh), flip(vh), flip(lgh[:, :, 1]), S0[:, 1])
    o = o_f + flip(o_b)
    o = o * lax.rsqrt(jnp.mean(o * o, axis=-1, keepdims=True) + EPS) * norm_w.astype(f32)
    out = o.reshape(B, L, GLA_DV) * jax.nn.silu(g.astype(f32))
    return out.astype(P.dtype), jnp.stack([S_f, S_b], axis=1).astype(P.dtype)


def channel_mixer(h, w_up, conv_w, w_down, rows, cols):
    u_gate, u_val = jnp.split(h @ w_up, 2, axis=-1)
    return (jax.nn.silu(dwconv_grid(u_gate, conv_w, rows, cols)) * u_val) @ w_down


def residual_block(x, mods, grid, S0, mixer, g_mix, g_ffn, w_up, conv_w, w_down):
    sh1, sc1, gt1, sh2, sc2, gt2 = mods
    y, S = mixer(modulate(rms_norm(x, g_mix), sh1, sc1), S0)
    x = x + gt1 * y
    h = modulate(rms_norm(x, g_ffn), sh2, sc2)
    x = x + gt2 * channel_mixer(h, w_up, conv_w, w_down, grid[0], grid[1])
    return x, S


def setup_inputs(seed: int = 0) -> dict:
    key = jax.random.key(seed)
    ks = iter(jax.random.split(key, 64))
    f32 = jnp.float32
    D = D_MODEL

    def nrm(shape, scale):
        return scale * jax.random.normal(next(ks), shape, f32)

    def near_one(shape):
        return 1.0 + nrm(shape, 0.05)

    def unif(shape, lo, hi):
        return jax.random.uniform(next(ks), shape, f32, lo, hi)

    return {
        'x_prompt': nrm((BATCH, SEQ, D), 1.0),
        'x_sample': nrm((DEC_BATCH, DEC_SEQ, D), 1.0),
        'state_rwkv': nrm((DEC_BATCH, N_EVEN, 2, RW_H, RW_HEAD, RW_HEAD), 0.5),
        'state_gla': nrm((DEC_BATCH, N_ODD, 2, GLA_H, GLA_HK, GLA_HV), 0.5),
        'c': nrm((DEC_BATCH, D), 1.0),
        'c_ctx': nrm((D,), 1.0),
        'w_ada': nrm((DEPTH, D, 6 * D), 0.5 * D ** -0.5),
        'b_ada': nrm((DEPTH, 6 * D), 0.02),
        'norm_mix': near_one((DEPTH, D)),
        'norm_ffn': near_one((DEPTH, D)),
        'ffn_w_up': nrm((DEPTH, D, 2 * D_FF), D ** -0.5),
        'ffn_conv': nrm((DEPTH, 3, 3, D_FF), 1.0 / 3.0),
        'ffn_w_down': nrm((DEPTH, D_FF, D), D_FF ** -0.5),
        'norm_final': near_one((D,)),
        'ev_w_in': nrm((N_EVEN, D, EV_IN), D ** -0.5),
        'ev_w_out': nrm((N_EVEN, D_HY + D_RW, D), (D_HY + D_RW) ** -0.5),
        'hy_short_w': nrm((N_EVEN, 3, 3 * D_HY), 3 ** -0.5),
        'hy_short_b': nrm((N_EVEN, 3 * D_HY), 0.02),
        'hy_w1': nrm((N_EVEN, HY_EMB, HY_FH), HY_EMB ** -0.5),
        'hy_b1': nrm((N_EVEN, HY_FH), 0.1),
        'hy_w2': nrm((N_EVEN, HY_FH, HY_FH), HY_FH ** -0.5),
        'hy_b2': nrm((N_EVEN, HY_FH), 0.1),
        'hy_w3': nrm((N_EVEN, HY_FH, 2 * D_HY), HY_FH ** -0.5),
        'hy_freq': near_one((N_EVEN, 2, HY_FH)),
        'hy_bias': nrm((N_EVEN, D_HY), 0.5),
        'rw_mu': unif((N_EVEN, RW_IN), 0.0, 1.0),
        'rw_w0': unif((N_EVEN, 2, D_RW), -6.0, 1.0),
        'rw_w2': nrm((N_EVEN, 2, RW_W_LORA, D_RW), 0.1 * RW_W_LORA ** -0.5),
        'rw_a0': nrm((N_EVEN, 2, D_RW), 0.1),
        'rw_a2': nrm((N_EVEN, 2, RW_A_LORA, D_RW), RW_A_LORA ** -0.5),
        'rw_g2': nrm((N_EVEN, RW_G_LORA, D_RW), RW_G_LORA ** -0.5),
        'rw_kk': 0.85 + nrm((N_EVEN, D_RW), 0.05),
        'rw_ka': near_one((N_EVEN, D_RW)),
        'rw_rk': nrm((N_EVEN, RW_H, RW_HEAD), 0.1),
        'rw_ln_w': near_one((N_EVEN, D_RW)),
        'rw_ln_b': nrm((N_EVEN, D_RW), 0.02),
        'od_w_in': nrm((N_ODD, D, OD_IN), D ** -0.5),
        'od_w_out': nrm((N_ODD, GLA_DV, D), GLA_DV ** -0.5),
        'gla_gk2': nrm((N_ODD, 2, GLA_GK_RANK, GLA_DK), GLA_GK_RANK ** -0.5),
        'gla_gk_b': nrm((N_ODD, 2, GLA_DK), 1.0),
        'gla_norm': near_one((N_ODD, GLA_HV)),
    }


def reference(x_prompt, x_sample, state_rwkv, state_gla, c, c_ctx, w_ada, b_ada, norm_mix, norm_ffn,
              ffn_w_up, ffn_conv, ffn_w_down, norm_final, ev_w_in, ev_w_out, hy_short_w, hy_short_b,
              hy_w1, hy_b1, hy_w2, hy_b2, hy_w3, hy_freq, hy_bias, rw_mu, rw_w0, rw_w2, rw_a0, rw_a2,
              rw_g2, rw_kk, rw_ka, rw_rk, rw_ln_w, rw_ln_b, od_w_in, od_w_out, gla_gk2, gla_gk_b, gla_norm):
    B_p, L_p, _ = x_prompt.shape
    L_s = x_sample.shape[1]
    grid_p = (1, L_p)
    grid_s = (L_s // GRID_W, GRID_W)
    xp, xs = x_prompt, x_sample
    rw_ctx_states = []
    gla_ctx_states = []
    for layer in range(DEPTH):
        mods_p = adaln(c_ctx[None, :], w_ada[layer], b_ada[layer])
        mods_s = adaln(c, w_ada[layer], b_ada[layer])
        if layer % 2 == 0:
            e = layer // 2

            def mixer(h, S0, e=e):
                P = h @ ev_w_in[e]
                y_hy = hyena_mixer(P[..., :3 * D_HY], hy_short_w[e], hy_short_b[e], hy_w1[e], hy_b1[e],
                                   hy_w2[e], hy_b2[e], hy_w3[e], hy_freq[e], hy_bias[e])
                y_rw, S = rwkv7_mixer(P[..., 3 * D_HY:], S0, rw_mu[e], rw_w0[e], rw_w2[e], rw_a0[e], rw_a2[e],
                                      rw_g2[e], rw_kk[e], rw_ka[e], rw_rk[e], rw_ln_w[e], rw_ln_b[e])
                return jnp.concatenate([y_hy, y_rw], axis=-1) @ ev_w_out[e], S

            S0_p = jnp.zeros((B_p, 2, RW_H, RW_HEAD, RW_HEAD), x_prompt.dtype)
            S0_s = state_rwkv[:, e]
            store = rw_ctx_states
        else:
            o = layer // 2

            def mixer(h, S0, o=o):
                y, S = gla_mixer(h @ od_w_in[o], S0, gla_gk2[o], gla_gk_b[o], gla_norm[o])
                return y @ od_w_out[o], S

            S0_p = jnp.zeros((B_p, 2, GLA_H, GLA_HK, GLA_HV), x_prompt.dtype)
            S0_s = state_gla[:, o]
            store = gla_ctx_states
        xp, S_ctx = residual_block(xp, mods_p, grid_p, S0_p, mixer, norm_mix[layer], norm_ffn[layer],
                                   ffn_w_up[layer], ffn_conv[layer], ffn_w_down[layer])
        xs, _ = residual_block(xs, mods_s, grid_s, S0_s, mixer, norm_mix[layer], norm_ffn[layer],
                               ffn_w_up[layer], ffn_conv[layer], ffn_w_down[layer])
        store.append(S_ctx)
    y_prompt = rms_norm(xp, norm_final)
    y_sample = rms_norm(xs, norm_final)
    new_state_rwkv = jnp.stack(rw_ctx_states, axis=1)
    new_state_gla = jnp.stack(gla_ctx_states, axis=1)
    return (y_prompt, y_sample, new_state_rwkv, new_state_gla)
```

```python
import functools
import math

import numpy as np
import jax
import jax.numpy as jnp
from jax import lax
from jax.experimental import pallas as pl
from jax.experimental.pallas import tpu as pltpu

F32 = jnp.float32
BF16 = jnp.bfloat16
HI = lax.Precision.HIGHEST

EPS = 1e-6
RW_HEAD = 64
RW_GN_EPS = 64e-5
GLA_H = 4
GLA_GATE_NORM = 16.0
GLA_CHUNK = 64
GRID_W = 64
HY_TARGET = 1e-2
HY_FAST = 0.3
HY_SLOW = 1.5

LANES = 128
SUBLANES = 8
VMEM_LIMIT = 56 * 1024 * 1024
FFT_MINOR = 128


def _cparams(*sem):
    return pltpu.CompilerParams(dimension_semantics=sem, vmem_limit_bytes=VMEM_LIMIT)


def _round_up(n, m):
    return (n + m - 1) // m * m


def _pow2_tile(n, pref):
    t = 1
    while t * 2 <= pref and n % (t * 2) == 0:
        t *= 2
    return t


def _col_tile(n, pref):
    if n % LANES:
        return n
    best = LANES
    for t in range(LANES, min(n, pref) + 1, LANES):
        if n % t == 0:
            best = t
    return best


class _Geo:
    def __init__(self, bp, lp, bs, ls, d):
        self.bp, self.lp, self.bs, self.ls, self.d = bp, lp, bs, ls, d
        self.np_, self.ns = bp * lp, bs * ls
        self.nt = self.np_ + self.ns
        self.tm = _pow2_tile(math.gcd(self.np_, ls), 512)
        assert lp & (lp - 1) == 0 and ls & (ls - 1) == 0, "sequence lengths must be powers of two"
        assert self.np_ % ls == 0, "sample rows must start on a sequence-length boundary"
        assert bs + 1 <= SUBLANES

    def modrow(self, i, tm):
        r0 = i * tm
        return jnp.where(r0 < self.np_, 0, 1 + (r0 - self.np_) // self.ls)

    def seq_len(self, i, tm):
        return jnp.where(i * tm < self.np_, self.lp, self.ls)


def _adaln_kernel(c_ref, w_ref, b_ref, o_ref):
    c = c_ref[...]
    s = c * jax.nn.sigmoid(c)
    o_ref[0] = jnp.dot(s, w_ref[0], precision=HI, preferred_element_type=F32) + b_ref[0]


def _adaln(cond, w_ada, b_ada):
    depth, d, n = w_ada.shape
    tn = _col_tile(n, 1024)
    return pl.pallas_call(
        _adaln_kernel,
        grid=(depth, n // tn),
        in_specs=[pl.BlockSpec((SUBLANES, d), lambda l, j: (0, 0)),
                  pl.BlockSpec((1, d, tn), lambda l, j: (l, 0, j)),
                  pl.BlockSpec((1, 1, tn), lambda l, j: (l, 0, j))],
        out_specs=pl.BlockSpec((1, SUBLANES, tn), lambda l, j: (l, 0, j)),
        out_shape=jax.ShapeDtypeStruct((depth, SUBLANES, n), F32),
        compiler_params=_cparams("parallel", "arbitrary"),
        name="adaln",
    )(cond, w_ada, b_ada.reshape(depth, 1, n))


def _norm_mm_kernel(x_ref, g_ref, sh_ref, sc_ref, w_ref, o_ref, xn_ref):
    @pl.when(pl.program_id(1) == 0)
    def _():
        x = x_ref[...]
        ms = jnp.mean(x * x, axis=-1, keepdims=True)
        y = x * lax.rsqrt(ms + EPS) * g_ref[...]
        xn_ref[...] = (y * (1.0 + sc_ref[0]) + sh_ref[0]).astype(BF16)

    o_ref[...] = jnp.dot(xn_ref[...], w_ref[...], preferred_element_type=F32).astype(o_ref.dtype)


def _norm_mm(geo, x, g, mods, which_shift, w, out_dtype=F32):
    nt, d = x.shape
    n = w.shape[1]
    tm = geo.tm
    tn = _col_tile(n, 512)

    def mod_map(which):
        return lambda i, j: (geo.modrow(i, tm) * 6 + which, 0, 0)

    return pl.pallas_call(
        _norm_mm_kernel,
        grid=(nt // tm, n // tn),
        in_specs=[pl.BlockSpec((tm, d), lambda i, j: (i, 0)),
                  pl.BlockSpec((1, d), lambda i, j: (0, 0)),
                  pl.BlockSpec((1, 1, d), mod_map(which_shift)),
                  pl.BlockSpec((1, 1, d), mod_map(which_shift + 1)),
                  pl.BlockSpec((d, tn), lambda i, j: (0, j))],
        out_specs=pl.BlockSpec((tm, tn), lambda i, j: (i, j)),
        out_shape=jax.ShapeDtypeStruct((nt, n), out_dtype),
        scratch_shapes=[pltpu.VMEM((tm, d), BF16)],
        compiler_params=_cparams("parallel", "arbitrary"),
        name="norm_mm",
    )(x, g.reshape(1, d), mods, mods, w)


def _res_mm_kernel(a_ref, w_ref, res_ref, gt_ref, o_ref):
    acc = jnp.dot(a_ref[...], w_ref[...], preferred_element_type=F32)
    o_ref[...] = res_ref[...] + gt_ref[0] * acc


def _res_mm(geo, a, w, res, mods, which_gate):
    nt, k = a.shape
    n = w.shape[1]
    tm = geo.tm
    tn = _col_tile(n, 512)
    return pl.pallas_call(
        _res_mm_kernel,
        grid=(nt // tm, n // tn),
        in_specs=[pl.BlockSpec((tm, k), lambda i, j: (i, 0)),
                  pl.BlockSpec((k, tn), lambda i, j: (0, j)),
                  pl.BlockSpec((tm, tn), lambda i, j: (i, j)),
                  pl.BlockSpec((1, 1, tn), lambda i, j: (geo.modrow(i, tm) * 6 + which_gate, 0, j))],
        out_specs=pl.BlockSpec((tm, tn), lambda i, j: (i, j)),
        out_shape=jax.ShapeDtypeStruct((nt, n), F32),
        compiler_params=_cparams("parallel", "arbitrary"),
        name="res_mm",
    )(a, w, res, mods)


def _final_norm_kernel(x_ref, g_ref, o_ref):
    x = x_ref[...]
    ms = jnp.mean(x * x, axis=-1, keepdims=True)
    o_ref[...] = x * lax.rsqrt(ms + EPS) * g_ref[...]


def _final_norm(geo, x, g):
    nt, d = x.shape
    tm = geo.tm
    return pl.pallas_call(
        _final_norm_kernel,
        grid=(nt // tm,),
        in_specs=[pl.BlockSpec((tm, d), lambda i: (i, 0)), pl.BlockSpec((1, d), lambda i: (0, 0))],
        out_specs=pl.BlockSpec((tm, d), lambda i: (i, 0)),
        out_shape=jax.ShapeDtypeStruct((nt, d), F32),
        compiler_params=_cparams("parallel"),
        name="final_norm",
    )(x, g.reshape(1, d))


def _seq_pos(geo, i, tm):
    lseq = geo.seq_len(i, tm)
    rid = lax.broadcasted_iota(jnp.int32, (tm, 1), 0) + i * tm
    return rid & (lseq - 1), lseq


def _neighbours(x, prev_row, next_row, tpos, lseq):
    tm = x.shape[0]
    rid = lax.broadcasted_iota(jnp.int32, (tm, 1), 0)
    xm = pltpu.roll(x, 1, axis=0)
    xm = jnp.where(rid == 0, prev_row, xm)
    xm = jnp.where(tpos == 0, 0.0, xm)
    xp = pltpu.roll(x, tm - 1, axis=0)
    xp = jnp.where(rid == tm - 1, next_row, xp)
    xp = jnp.where(tpos == lseq - 1, 0.0, xp)
    return xm, xp


def _halo_specs(nt, tm, width, col):
    r = tm // SUBLANES
    last = nt // SUBLANES - 1
    prev = pl.BlockSpec((SUBLANES, width), lambda i, j: (jnp.maximum(i * r - 1, 0), col(j)))
    nxt = pl.BlockSpec((SUBLANES, width), lambda i, j: (jnp.minimum((i + 1) * r, last), col(j)))
    return prev, nxt


def _hy_prep_kernel(geo, tm, *refs):
    (x0_ref, x0p_ref, x0n_ref, x1_ref, x1p_ref, x1n_ref, x2_ref, x2p_ref, x2n_ref,
     w_ref, b_ref, o0_ref, u_ref) = refs
    i = pl.program_id(0)
    tpos, lseq = _seq_pos(geo, i, tm)

    def conv(x_ref, p_ref, n_ref, g):
        x = x_ref[...]
        xm, xp = _neighbours(x, p_ref[SUBLANES - 1:SUBLANES, :], n_ref[0:1, :], tpos, lseq)
        return xm * w_ref[0, g] + x * w_ref[1, g] + xp * w_ref[2, g] + b_ref[g]

    c0 = conv(x0_ref, x0p_ref, x0n_ref, 0)
    c1 = conv(x1_ref, x1p_ref, x1n_ref, 1)
    c2 = conv(x2_ref, x2p_ref, x2n_ref, 2)
    o0_ref[...] = c0
    u_ref[...] = c1 * c2


def _hy_prep(geo, p_hy, short_w, short_b):
    nt = p_hy.shape[0]
    dh = p_hy.shape[1] // 3
    tm = _pow2_tile(geo.tm, 256)
    tc = _col_tile(dh, 512)
    ncb = dh // tc
    specs = []
    for g in range(3):
        col = (lambda j, g=g: g * ncb + j)
        specs.append(pl.BlockSpec((tm, tc), lambda i, j, col=col: (i, col(j))))
        specs.extend(_halo_specs(nt, tm, tc, col))
    specs.append(pl.BlockSpec((3, 3, 1, tc), lambda i, j: (0, 0, 0, j)))
    specs.append(pl.BlockSpec((3, 1, tc), lambda i, j: (0, 0, j)))
    out_spec = pl.BlockSpec((tm, tc), lambda i, j: (i, j))
    w4 = short_w.reshape(3, 3, 1, dh)
    b3 = short_b.reshape(3, 1, dh)
    return pl.pallas_call(
        functools.partial(_hy_prep_kernel, geo, tm),
        grid=(nt // tm, ncb),
        in_specs=specs,
        out_specs=[out_spec, out_spec],
        out_shape=[jax.ShapeDtypeStruct((nt, dh), F32)] * 2,
        compiler_params=_cparams("parallel", "arbitrary"),
        name="hy_prep",
    )(*([p_hy] * 9), w4, b3)


def _lmm_kernel(has_epi, *refs):
    if has_epi:
        a_ref, x_ref, x0_ref, u_ref, bias_ref, o_ref = refs
    else:
        a_ref, x_ref, o_ref = refs
    acc = jnp.dot(a_ref[0], x_ref[0], precision=HI, preferred_element_type=F32)
    if has_epi:
        acc = x0_ref[0] * (acc + u_ref[0] * bias_ref[...])
    o_ref[0] = acc.astype(o_ref.dtype)


def _lmm(a, x, epi=None, out_dtype=F32):
    gm, m, k = a.shape
    g, _, n = x.shape
    tn = _col_tile(n, 2048)
    xspec = pl.BlockSpec((1, k, tn), lambda gi, j: (gi, 0, j))
    ospec = pl.BlockSpec((1, m, tn), lambda gi, j: (gi, 0, j))
    specs = [pl.BlockSpec((1, m, k), lambda gi, j: (gi % gm, 0, 0)), xspec]
    args = [a, x]
    if epi is not None:
        x0, u, bias = epi
        specs += [ospec, ospec, pl.BlockSpec((1, tn), lambda gi, j: (0, j))]
        args += [x0, u, bias]
    return pl.pallas_call(
        functools.partial(_lmm_kernel, epi is not None),
        grid=(g, n // tn),
        in_specs=specs,
        out_specs=ospec,
        out_shape=jax.ShapeDtypeStruct((g, m, n), out_dtype),
        compiler_params=_cparams("parallel", "arbitrary"),
        name="dft_lmm",
    )(*args)


def _spec_kernel(has_epi, *refs):
    if has_epi:
        mf_ref, h_ref, mi_ref, x_ref, x0_ref, bias_ref, o_ref = refs
    else:
        mf_ref, h_ref, mi_ref, x_ref, o_ref = refs
    x = x_ref[0]
    f = jnp.dot(mf_ref[0], x, precision=HI, preferred_element_type=F32)
    r = f.shape[0] // 2
    fr, fi = f[:r], f[r:]
    hr, hi = h_ref[0, :r], h_ref[0, r:]
    y = jnp.concatenate([fr * hr - fi * hi, fr * hi + fi * hr], axis=0)
    out = jnp.dot(mi_ref[0], y, precision=HI, preferred_element_type=F32)
    if has_epi:
        out = x0_ref[0] * (out + x * bias_ref[...])
    o_ref[0] = out.astype(o_ref.dtype)


def _spectral(mf, h, mi, x, epi=None, out_dtype=F32):
    gm, r2, k = mf.shape
    kout = mi.shape[1]
    g, _, c = x.shape
    tc = _col_tile(c, 512)
    xspec = pl.BlockSpec((1, k, tc), lambda gi, j: (gi, 0, j))
    ospec = pl.BlockSpec((1, kout, tc), lambda gi, j: (gi, 0, j))
    specs = [pl.BlockSpec((1, r2, k), lambda gi, j: (gi % gm, 0, 0)),
             pl.BlockSpec((1, r2, tc), lambda gi, j: (gi % gm, 0, j)),
             pl.BlockSpec((1, kout, r2), lambda gi, j: (gi % gm, 0, 0)),
             xspec]
    args = [mf, h, mi, x]
    if epi is not None:
        x0, bias = epi
        specs += [ospec, pl.BlockSpec((1, tc), lambda gi, j: (0, j))]
        args += [x0, bias]
    return pl.pallas_call(
        functools.partial(_spec_kernel, epi is not None),
        grid=(g, c // tc),
        in_specs=specs,
        out_specs=ospec,
        out_shape=jax.ShapeDtypeStruct((g, kout, c), out_dtype),
        compiler_params=_cparams("parallel", "arbitrary"),
        name="dft_spectral",
    )(*args)


def _dft_tables_direct(l):
    n = 2 * l
    nf = l + 1
    r = _round_up(nf, SUBLANES)
    kk = np.arange(nf)[:, None].astype(np.float64)
    ang_half = 2.0 * np.pi * ((kk * np.arange(l)[None, :]) % n) / n
    ang_full = 2.0 * np.pi * ((kk * np.arange(n)[None, :]) % n) / n

    def fwd(ang):
        m = np.zeros((2 * r, ang.shape[1]))
        m[:nf] = np.cos(ang)
        m[r:r + nf] = -np.sin(ang)
        return m

    coef = np.full((nf,), 2.0 / n)
    coef[0] = coef[-1] = 1.0 / n
    inv = np.zeros((l, 2 * r))
    inv[:, :nf] = (np.cos(ang_half) * coef[:, None]).T
    inv[:, r:r + nf] = (-np.sin(ang_half) * coef[:, None]).T
    f32 = lambda a: jnp.asarray(a[None], F32)
    return f32(fwd(ang_half)), f32(fwd(ang_full)), f32(inv)


def _dft_tables_two_level(l):
    n = 2 * l
    n1 = FFT_MINOR
    n2 = n // n1
    nb = n2 // 2 + 1
    nbp = _round_up(nb, 4)
    k2 = np.arange(nb)[:, None].astype(np.float64)
    ang2 = 2.0 * np.pi * ((k2 * np.arange(n2)[None, :]) % n2) / n2
    ms_full = np.zeros((2 * nbp, n2))
    ms_full[0:2 * nb:2] = np.cos(ang2)
    ms_full[1:2 * nb:2] = -np.sin(ang2)
    ms_half = ms_full[:, :n2 // 2]
    coef = np.full((nb,), 2.0 / n)
    coef[0] = coef[-1] = 1.0 / n
    ms_inv = np.zeros((n2 // 2, 2 * nbp))
    ms_inv[:, 0:2 * nb:2] = (np.cos(ang2[:, :n2 // 2]) * coef[:, None]).T
    ms_inv[:, 1:2 * nb:2] = (-np.sin(ang2[:, :n2 // 2]) * coef[:, None]).T

    a_fwd = np.zeros((nbp, 2 * n1, 2 * n1))
    a_inv = np.zeros((nbp, 2 * n1, 2 * n1))
    i1 = np.arange(n1).astype(np.float64)
    for b in range(nb):
        ph = 2.0 * np.pi * (((np.outer(i1, i1) * n2) + (i1[None, :] * b)) % n) / n
        wr, wi = np.cos(ph), -np.sin(ph)
        a_fwd[b] = np.block([[wr, -wi], [wi, wr]])
        ph = 2.0 * np.pi * (((np.outer(i1, i1) * n2) + (i1[:, None] * b)) % n) / n
        pr, pi_ = np.cos(ph), np.sin(ph)
        a_inv[b] = np.block([[pr, -pi_], [pi_, pr]])
    f32 = lambda a: jnp.asarray(a, F32)
    return (f32(ms_half[None]), f32(ms_full[None]), f32(ms_inv[None]), f32(a_fwd), f32(a_inv), n1, n2, nbp)


def _hyena_filter(l, w1, b1, w2, b2, w3, freq, dh):
    pos = jnp.arange(l, dtype=F32)[:, None]
    t = pos / max(l - 1, 1)
    emb = w1.shape[0]
    bands = (emb - 1) // 2
    fb = jnp.linspace(1e-4, bands - 1, bands, dtype=F32)
    ang = (2.0 * math.pi / l) * pos * fb
    z = jnp.concatenate([t, jnp.cos(ang), -jnp.sin(ang)], axis=-1)
    h = jnp.sin(freq[0] * (jnp.dot(z, w1, precision=HI) + b1))
    h = jnp.sin(freq[1] * (jnp.dot(h, w2, precision=HI) + b2))
    h = jnp.dot(h, w3, precision=HI).reshape(l, 2, dh)
    deltas = jnp.abs(jnp.linspace(math.log(HY_TARGET) / HY_SLOW, math.log(HY_TARGET) / HY_FAST, dh, dtype=F32))
    h = h * jnp.exp(-t[:, :, None] * deltas)
    filt = jnp.concatenate([h[:, 0], jnp.zeros((1, dh), F32), jnp.flip(h[1:, 1], axis=0)], axis=0)
    return filt / jnp.sum(jnp.abs(filt), axis=0, keepdims=True)


def _hyena_long_conv(geo, x0, u, filt_p, filt_s, bias):
    dh = u.shape[1]
    bias2 = bias.reshape(1, dh)
    mf_half, mf_full, mi = _dft_tables_direct(geo.lp)
    h_p = _lmm(mf_full, filt_p[None])
    up = u[:geo.np_].reshape(geo.bp, geo.lp, dh)
    x0p = x0[:geo.np_].reshape(geo.bp, geo.lp, dh)
    y_p = _spectral(mf_half, h_p, mi, up, epi=(x0p, bias2), out_dtype=BF16)
    ms_half, ms_full, ms_inv, a_fwd, a_inv, n1, n2, nbp = _dft_tables_two_level(geo.ls)
    hs = _lmm(ms_full, filt_s.reshape(1, n2, n1 * dh))
    hs = _lmm(a_fwd, hs.reshape(nbp, 2 * n1, dh))
    us = u[geo.np_:].reshape(geo.bs, n2 // 2, n1 * dh)
    x0s = x0[geo.np_:].reshape(geo.bs, n2 // 2, n1 * dh)
    xs = _lmm(ms_half, us)
    zs = _spectral(a_fwd, hs, a_inv, xs.reshape(geo.bs * nbp, 2 * n1, dh))
    zs = zs.reshape(geo.bs, 2 * nbp, n1 * dh)
    y_s = _lmm(ms_inv, zs, epi=(x0s, us, jnp.tile(bias2, (1, n1))), out_dtype=BF16)
    return jnp.concatenate([y_p.reshape(geo.np_, dh), y_s.reshape(geo.ns, dh)], axis=0)


def _seg_sum(x, e_ref, et_ref):
    s = jnp.dot(x, e_ref[...], precision=HI, preferred_element_type=F32)
    return jnp.dot(s, et_ref[...], precision=HI, preferred_element_type=F32)


def _rw_prep_kernel(geo, tm, dr, *refs):
    (x_ref, xp_ref, xn_ref, mu_ref, w0_ref, w2_ref, a0_ref, a2_ref, g2_ref, kk_ref, ka_ref, rk_ref,
     e_ref, et_ref,
     r_o, v_o, nkk_o, w0_o, w1_o, kx0_o, kx1_o, b0_o, b1_o, g_o, bon_o) = refs
    i = pl.program_id(0)
    tpos, lseq = _seq_pos(geo, i, tm)
    x = x_ref[...]
    xm, xp = _neighbours(x, xp_ref[SUBLANES - 1:SUBLANES, :], xn_ref[0:1, :], tpos, lseq)
    x = x + mu_ref[...] * (0.5 * (xm + xp) - x)
    r = x[:, 0:dr]
    k = x[:, dr:2 * dr]
    v = x[:, 2 * dr:3 * dr]
    wd = x[:, 3 * dr:3 * dr + LANES]
    ad = x[:, 3 * dr + LANES:3 * dr + 2 * LANES]
    gd = x[:, 3 * dr + 2 * LANES:3 * dr + 3 * LANES]
    wl = w0_ref[...] + jnp.dot(jnp.tanh(wd), w2_ref[...], precision=HI, preferred_element_type=F32)
    w_log = -jax.nn.softplus(-wl) - 0.5
    decay = jnp.exp(-jnp.exp(w_log))
    a = jax.nn.sigmoid(a0_ref[...] + jnp.dot(ad, a2_ref[...], precision=HI, preferred_element_type=F32))
    g_o[...] = jnp.dot(jax.nn.sigmoid(gd), g2_ref[...], precision=HI, preferred_element_type=F32)
    kk = k * kk_ref[...]
    kk = kk * lax.rsqrt(_seg_sum(kk * kk, e_ref, et_ref) + 1e-12)
    a_0, a_1 = a[:, :dr], a[:, dr:]
    kx0 = k * (1.0 + (a_0 - 1.0) * ka_ref[...])
    kx1 = k * (1.0 + (a_1 - 1.0) * ka_ref[...])
    r_o[...] = r
    v_o[...] = v
    nkk_o[...] = -kk
    w0_o[...] = decay[:, :dr]
    w1_o[...] = decay[:, dr:]
    kx0_o[...] = kx0
    kx1_o[...] = kx1
    b0_o[...] = kk * a_0
    b1_o[...] = kk * a_1
    bon_o[...] = _seg_sum(r * (kx0 + kx1) * rk_ref[...], e_ref, et_ref) * v


def _blockdiag2(w):
    _, r, c = w.shape
    z = jnp.zeros((r, c), w.dtype)
    return jnp.concatenate([jnp.concatenate([w[0], z], axis=1), jnp.concatenate([z, w[1]], axis=1)], axis=0)


def _head_indicator(dr):
    h = dr // RW_HEAD
    hp = _round_up(h, LANES)
    e = np.zeros((dr, hp), np.float32)
    e[np.arange(dr), np.arange(dr) // RW_HEAD] = 1.0
    return jnp.asarray(e), jnp.asarray(e.T.copy())


def _rw_prep(geo, p_rw, mu, w0, w2, a0, a2, g2, k_k, k_a, r_k):
    nt, width = p_rw.shape
    dr = k_k.shape[0]
    assert 2 * w2.shape[1] == LANES and 2 * a2.shape[1] == LANES and g2.shape[0] == LANES
    tm = _pow2_tile(geo.tm, 128)
    full = lambda shape: pl.BlockSpec(shape, lambda i, j: tuple(0 for _ in shape))
    prev, nxt = _halo_specs(nt, tm, width, lambda j: 0)
    e, et = _head_indicator(dr)
    consts = [mu.reshape(1, width), w0.reshape(1, 2 * dr), _blockdiag2(w2), a0.reshape(1, 2 * dr),
              _blockdiag2(a2), g2, k_k.reshape(1, dr), k_a.reshape(1, dr), r_k.reshape(1, dr), e, et]
    ospec = pl.BlockSpec((tm, dr), lambda i, j: (i, 0))
    return pl.pallas_call(
        functools.partial(_rw_prep_kernel, geo, tm, dr),
        grid=(nt // tm, 1),
        in_specs=[pl.BlockSpec((tm, width), lambda i, j: (i, 0)), prev, nxt] + [full(c.shape) for c in consts],
        out_specs=[ospec] * 11,
        out_shape=[jax.ShapeDtypeStruct((nt, dr), F32)] * 11,
        compiler_params=_cparams("parallel", "arbitrary"),
        name="rw_prep",
    )(p_rw, p_rw, p_rw, *consts)


def _rw_scan_kernel(tc, r_ref, nkk_ref, v_ref, w_ref, kx_ref, b_ref, s0_ref, y_ref, so_ref, s_ref):
    t_chunk = pl.program_id(1)

    @pl.when(t_chunk == 0)
    def _():
        s_ref[...] = s0_ref[...]

    nacc = 4

    def step(t, carry):
        acc = [jnp.zeros((RW_HEAD, LANES), F32) for _ in range(nacc)]
        for k in range(RW_HEAD):
            acc[k % nacc] = acc[k % nacc] + s_ref[k] * nkk_ref[t, k:k + 1, :]
        sa = (acc[0] + acc[1]) + (acc[2] + acc[3])
        vv = v_ref[t]
        acc = [jnp.zeros((RW_HEAD, LANES), F32) for _ in range(nacc)]
        for k in range(RW_HEAD):
            new = s_ref[k] * w_ref[t, k:k + 1, :] + sa * b_ref[t, k:k + 1, :] + vv * kx_ref[t, k:k + 1, :]
            s_ref[k] = new
            acc[k % nacc] = acc[k % nacc] + new * r_ref[t, k:k + 1, :]
        y_ref[t] = (acc[0] + acc[1]) + (acc[2] + acc[3])
        return carry

    lax.fori_loop(0, tc, step, 0)

    @pl.when(t_chunk == pl.num_programs(1) - 1)
    def _():
        so_ref[...] = s_ref[...]


def _rw_scan(r, nkk, v, w, kx, b, s0):
    t_len, n, lanes = r.shape
    tc = _pow2_tile(t_len, 32)
    xspec = pl.BlockSpec((tc, n, LANES), lambda g, t: (t, 0, g))
    sspec = pl.BlockSpec((n, n, LANES), lambda g, t: (0, 0, g))
    return pl.pallas_call(
        functools.partial(_rw_scan_kernel, tc),
        grid=(lanes // LANES, t_len // tc),
        in_specs=[xspec] * 6 + [sspec],
        out_specs=[xspec, sspec],
        out_shape=[jax.ShapeDtypeStruct((t_len, n, lanes), F32), jax.ShapeDtypeStruct((n, n, lanes), F32)],
        scratch_shapes=[pltpu.VMEM((n, n, LANES), F32)],
        compiler_params=_cparams("parallel", "arbitrary"),
        name="rw_scan",
    )(r, nkk, v, w, kx, b, s0)


def _to_streams(x0, x1, b, l, h):
    def t(x):
        return x.reshape(b, l, h, RW_HEAD).transpose(1, 3, 0, 2).reshape(l, RW_HEAD, b * h)
    out = jnp.concatenate([t(x0), jnp.flip(t(x1), axis=0)], axis=-1)
    pad = _round_up(2 * b * h, LANES) - 2 * b * h
    return jnp.pad(out, ((0, 0), (0, 0), (0, pad))) if pad else out


def _from_streams(y, b, l, h):
    def t(x):
        return x.reshape(l, RW_HEAD, b, h).transpose(2, 0, 3, 1).reshape(b * l, h * RW_HEAD)
    return t(y[:, :, :b * h]), t(jnp.flip(y[:, :, b * h:2 * b * h], axis=0))


def _rw_post_kernel(yf_ref, yb_ref, bon_ref, g_ref, lnw_ref, lnb_ref, e_ref, et_ref, o_ref):
    y = yf_ref[...] + yb_ref[...]
    inv_n = 1.0 / RW_HEAD
    mean = _seg_sum(y, e_ref, et_ref) * inv_n
    yc = y - mean
    var = _seg_sum(yc * yc, e_ref, et_ref) * inv_n
    yn = yc * lax.rsqrt(var + RW_GN_EPS) * lnw_ref[...] + lnb_ref[...]
    o_ref[...] = ((yn + bon_ref[...]) * g_ref[...]).astype(o_ref.dtype)


def _rw_post(geo, yf, yb, bonus, g, ln_w, ln_b):
    nt, dr = yf.shape
    tm = _pow2_tile(geo.tm, 256)
    e, et = _head_indicator(dr)
    tspec = pl.BlockSpec((tm, dr), lambda i: (i, 0))
    full = lambda a: pl.BlockSpec(a.shape, lambda i: (0, 0))
    consts = [ln_w.reshape(1, dr), ln_b.reshape(1, dr), e, et]
    return pl.pallas_call(
        _rw_post_kernel,
        grid=(nt // tm,),
        in_specs=[tspec] * 4 + [full(c) for c in consts],
        out_specs=tspec,
        out_shape=jax.ShapeDtypeStruct((nt, dr), BF16),
        compiler_params=_cparams("parallel"),
        name="rw_post",
    )(yf, yb, bonus, g, *consts)


def _rwkv_mixer(geo, p_rw, s0_sample, mu, w0, w2, a0, a2, g2, k_k, k_a, r_k, ln_w, ln_b):
    dr = k_k.shape[0]
    h = dr // RW_HEAD
    r, v, nkk, w_0, w_1, kx0, kx1, b_0, b_1, g, bonus = _rw_prep(geo, p_rw, mu, w0, w2, a0, a2, g2, k_k, k_a, r_k)
    outs = []
    states = None
    for part in range(2):
        if part == 0:
            sl, b, l = slice(0, geo.np_), geo.bp, geo.lp
        else:
            sl, b, l = slice(geo.np_, geo.nt), geo.bs, geo.ls
        lanes = _round_up(2 * b * h, LANES)
        if part == 0:
            s0 = jnp.zeros((RW_HEAD, RW_HEAD, lanes), F32)
        else:
            s0 = s0_sample.transpose(4, 3, 1, 0, 2).reshape(RW_HEAD, RW_HEAD, 2 * b * h)
            s0 = jnp.pad(s0, ((0, 0), (0, 0), (0, lanes - 2 * b * h)))
        y, s_fin = _rw_scan(_to_streams(r[sl], r[sl], b, l, h), _to_streams(nkk[sl], nkk[sl], b, l, h),
                            _to_streams(v[sl], v[sl], b, l, h), _to_streams(w_0[sl], w_1[sl], b, l, h),
                            _to_streams(kx0[sl], kx1[sl], b, l, h), _to_streams(b_0[sl], b_1[sl], b, l, h), s0)
        outs.append(_from_streams(y, b, l, h))
        if part == 0:
            states = s_fin[:, :, :2 * b * h].reshape(RW_HEAD, RW_HEAD, 2, b, h).transpose(3, 2, 4, 1, 0)
    yf = jnp.concatenate([outs[0][0], outs[1][0]], axis=0)
    yb = jnp.concatenate([outs[0][1], outs[1][1]], axis=0)
    return _rw_post(geo, yf, yb, bonus, g, ln_w, ln_b), states


def _gla_dir(q, k, v, gkd, gk2, gkb, s, reverse):
    c = q.shape[0]
    z = jnp.dot(gkd, gk2, precision=HI, preferred_element_type=F32) + gkb
    lg = (jnp.minimum(z, 0.0) - jnp.log1p(jnp.exp(-jnp.abs(z)))) * (1.0 / GLA_GATE_NORM)
    row = lax.broadcasted_iota(jnp.int32, (c, c), 0)
    col = lax.broadcasted_iota(jnp.int32, (c, c), 1)
    tri = (col >= row) if reverse else (col <= row)
    b = jnp.dot(tri.astype(F32), lg, precision=HI, preferred_element_type=F32)
    i_ref = c // 2 if reverse else c // 2 - 1
    i_last = 0 if reverse else c - 1
    b_ref = b[i_ref:i_ref + 1, :]
    b_last = b[i_last:i_last + 1, :]
    scores = lax.dot_general(q * jnp.exp(b - b_ref), k * jnp.exp(b_ref - b), (((1,), (1,)), ((), ())),
                             precision=HI, preferred_element_type=F32)
    scores = jnp.where(tri, scores, 0.0)
    o = jnp.dot(scores, v, precision=HI, preferred_element_type=F32)
    o = o + jnp.dot(q * jnp.exp(b), s, precision=HI, preferred_element_type=F32)
    bt = b.T
    bt_last = bt[:, i_last:i_last + 1]
    k_dec_t = k.T * jnp.exp(bt_last - bt)
    s_new = s * jnp.exp(bt_last) + jnp.dot(k_dec_t, v, precision=HI, preferred_element_type=F32)
    return o, s_new


def _gla_kernel(scale, *refs):
    (qf_ref, kf_ref, vf_ref, df_ref, qb_ref, kb_ref, vb_ref, db_ref, gk2_ref, gkb_ref, s0_ref,
     of_ref, ob_ref, so_ref, sf, sb) = refs
    ci = pl.program_id(2)

    @pl.when(ci == 0)
    def _():
        sf[...] = s0_ref[0, 0, 0]
        sb[...] = s0_ref[0, 1, 0]

    rank = gk2_ref.shape[1]
    o, s_new = _gla_dir(qf_ref[...] * scale, kf_ref[...], vf_ref[...], df_ref[:, 0:rank],
                        gk2_ref[0], gkb_ref[0], sf[...], False)
    of_ref[...] = o
    sf[...] = s_new
    o, s_new = _gla_dir(qb_ref[...] * scale, kb_ref[...], vb_ref[...], db_ref[:, rank:2 * rank],
                        gk2_ref[1], gkb_ref[1], sb[...], True)
    ob_ref[...] = o
    sb[...] = s_new

    @pl.when(ci == pl.num_programs(2) - 1)
    def _():
        so_ref[0, 0, 0] = sf[...]
        so_ref[0, 1, 0] = sb[...]


def _gla_scan(p, gkd, row0, b, l, dk, dv, gk2, gk_b, s0):
    hk, hv = dk // GLA_H, dv // GLA_H
    c = GLA_CHUNK
    n = l // c
    blk0 = row0 // c
    rank = gk2.shape[1]

    def tok(bi, ci, rev):
        return blk0 + bi * n + (n - 1 - ci if rev else ci)

    def specs(rev):
        return [pl.BlockSpec((c, hk), lambda bi, hi, ci: (tok(bi, ci, rev), hi)),
                pl.BlockSpec((c, hk), lambda bi, hi, ci: (tok(bi, ci, rev), GLA_H + hi)),
                pl.BlockSpec((c, hv), lambda bi, hi, ci: (tok(bi, ci, rev), 2 * dk // hv + hi)),
                pl.BlockSpec((c, LANES), lambda bi, hi, ci: (tok(bi, ci, rev), 0))]

    def ospec(rev):
        return pl.BlockSpec((c, hv), lambda bi, hi, ci: (bi * n + (n - 1 - ci if rev else ci), hi))

    sspec = pl.BlockSpec((1, 2, 1, hk, hv), lambda bi, hi, ci: (bi, 0, hi, 0, 0))
    return pl.pallas_call(
        functools.partial(_gla_kernel, float(hk) ** -0.5),
        grid=(b, GLA_H, n),
        in_specs=specs(False) + specs(True) + [
            pl.BlockSpec((2, rank, hk), lambda bi, hi, ci: (0, 0, hi)),
            pl.BlockSpec((2, 1, hk), lambda bi, hi, ci: (0, 0, hi)),
            sspec],
        out_specs=[ospec(False), ospec(True), sspec],
        out_shape=[jax.ShapeDtypeStruct((b * l, dv), F32), jax.ShapeDtypeStruct((b * l, dv), F32),
                   jax.ShapeDtypeStruct((b, 2, GLA_H, hk, hv), F32)],
        scratch_shapes=[pltpu.VMEM((hk, hv), F32), pltpu.VMEM((hk, hv), F32)],
        compiler_params=_cparams("parallel", "parallel", "arbitrary"),
        name="gla_scan",
    )(p, p, p, gkd, p, p, p, gkd, gk2, gk_b.reshape(2, 1, dk), s0)


def _gla_post_kernel(hv, of_ref, ob_ref, g_ref, nw_ref, o_ref):
    o = of_ref[...] + ob_ref[...]
    g = g_ref[...]
    for h in range(GLA_H):
        oh = o[:, h * hv:(h + 1) * hv]
        gh = g[:, h * hv:(h + 1) * hv]
        oh = oh * lax.rsqrt(jnp.mean(oh * oh, axis=-1, keepdims=True) + EPS) * nw_ref[...]
        o_ref[:, h * hv:(h + 1) * hv] = (oh * (gh * jax.nn.sigmoid(gh))).astype(o_ref.dtype)


def _gla_post(geo, o_f, o_b, p, dk, dv, norm_w):
    nt = o_f.shape[0]
    tm = _pow2_tile(geo.tm, 256)
    hv = dv // GLA_H
    tspec = pl.BlockSpec((tm, dv), lambda i: (i, 0))
    return pl.pallas_call(
        functools.partial(_gla_post_kernel, hv),
        grid=(nt // tm,),
        in_specs=[tspec, tspec, pl.BlockSpec((tm, dv), lambda i: (i, 2 * dk // dv + 1)),
                  pl.BlockSpec((1, hv), lambda i: (0, 0))],
        out_specs=tspec,
        out_shape=jax.ShapeDtypeStruct((nt, dv), BF16),
        compiler_params=_cparams("parallel"),
        name="gla_post",
    )(o_f, o_b, p, norm_w.reshape(1, hv))


def _gla_mixer(geo, p, gkd, s0_sample, gk2, gk_b, norm_w, dk, dv):
    hk, hv = dk // GLA_H, dv // GLA_H
    s0_p = jnp.zeros((geo.bp, 2, GLA_H, hk, hv), F32)
    of_p, ob_p, s_p = _gla_scan(p, gkd, 0, geo.bp, geo.lp, dk, dv, gk2, gk_b, s0_p)
    of_s, ob_s, _ = _gla_scan(p, gkd, geo.np_, geo.bs, geo.ls, dk, dv, gk2, gk_b, s0_sample)
    o_f = jnp.concatenate([of_p, of_s], axis=0)
    o_b = jnp.concatenate([ob_p, ob_s], axis=0)
    return _gla_post(geo, o_f, o_b, p, dk, dv, norm_w), s_p


def _conv_gate_kernel(geo, tm, g_ref, gp_ref, gn_ref, v_ref, w_ref, o_ref):
    i = pl.program_id(0)
    tpos, lseq = _seq_pos(geo, i, tm)
    is_prompt = i * tm < geo.np_
    log_w = jnp.where(is_prompt, int(math.log2(geo.lp)), int(math.log2(GRID_W)))
    col = tpos & ((1 << log_w) - 1)
    row = tpos >> log_w
    ncol = 1 << log_w
    nrow = lseq >> log_w
    halo = GRID_W
    ext = jnp.concatenate([gp_ref[...], g_ref[...], gn_ref[...]], axis=0)
    n_ext = tm + 2 * halo
    shifted = {-1: pltpu.roll(ext, 1, axis=0), 0: ext, 1: pltpu.roll(ext, n_ext - 1, axis=0)}
    acc = jnp.zeros(g_ref.shape, F32)
    for di in (-1, 0, 1):
        row_ok = (row + di >= 0) & (row + di < nrow)
        for dj in (-1, 0, 1):
            ok = row_ok & (col + dj >= 0) & (col + dj < ncol)
            start = halo + di * GRID_W
            tap = shifted[dj][start:start + tm, :]
            acc = acc + jnp.where(ok, tap, 0.0) * w_ref[di + 1, dj + 1]
    o_ref[...] = (acc * jax.nn.sigmoid(acc) * v_ref[...]).astype(o_ref.dtype)


def _conv_gate(geo, u, conv_w):
    nt = u.shape[0]
    f = u.shape[1] // 2
    tm = _pow2_tile(geo.tm, 256)
    assert tm % GRID_W == 0
    tc = _col_tile(f, 512)
    ncb = f // tc
    r = tm // GRID_W
    last = nt // GRID_W - 1
    return pl.pallas_call(
        functools.partial(_conv_gate_kernel, geo, tm),
        grid=(nt // tm, ncb),
        in_specs=[pl.BlockSpec((tm, tc), lambda i, j: (i, j)),
                  pl.BlockSpec((GRID_W, tc), lambda i, j: (jnp.maximum(i * r - 1, 0), j)),
                  pl.BlockSpec((GRID_W, tc), lambda i, j: (jnp.minimum((i + 1) * r, last), j)),
                  pl.BlockSpec((tm, tc), lambda i, j: (i, ncb + j)),
                  pl.BlockSpec((3, 3, 1, tc), lambda i, j: (0, 0, 0, j))],
        out_specs=pl.BlockSpec((tm, tc), lambda i, j: (i, j)),
        out_shape=jax.ShapeDtypeStruct((nt, f), BF16),
        compiler_params=_cparams("parallel", "arbitrary"),
        name="conv_gate",
    )(u, u, u, u, conv_w.reshape(3, 3, 1, f))


def _pad_cols(w, mult):
    n = w.shape[-1]
    pad = _round_up(n, mult) - n
    return jnp.pad(w, [(0, 0)] * (w.ndim - 1) + [(0, pad)]) if pad else w


def kernel(x_prompt, x_sample, state_rwkv, state_gla, c, c_ctx, w_ada, b_ada, norm_mix, norm_ffn, ffn_w_up, ffn_conv, ffn_w_down, norm_final, ev_w_in, ev_w_out, hy_short_w, hy_short_b, hy_w1, hy_b1, hy_w2, hy_b2, hy_w3, hy_freq, hy_bias, rw_mu, rw_w0, rw_w2, rw_a0, rw_a2, rw_g2, rw_kk, rw_ka, rw_rk, rw_ln_w, rw_ln_b, od_w_in, od_w_out, gla_gk2, gla_gk_b, gla_norm):
    bp, lp, d = x_prompt.shape
    bs, ls, _ = x_sample.shape
    geo = _Geo(bp, lp, bs, ls, d)
    depth = w_ada.shape[0]
    x = jnp.concatenate([x_prompt.reshape(geo.np_, d), x_sample.reshape(geo.ns, d)], axis=0)

    cond = jnp.concatenate([c_ctx[None, :], c], axis=0)
    cond = jnp.pad(cond, ((0, SUBLANES - cond.shape[0]), (0, 0)))
    mods_all = _adaln(cond, w_ada, b_ada)

    rw_states, gla_states = [], []
    for layer in range(depth):
        mods = mods_all[layer].reshape(SUBLANES * 6, 1, d)
        if layer % 2 == 0:
            e = layer // 2
            dh = hy_bias.shape[1]
            dr = rw_kk.shape[1]
            w_hy = ev_w_in[e][:, :3 * dh].astype(BF16)
            w_rw = _pad_cols(ev_w_in[e][:, 3 * dh:], 512).astype(BF16)
            p_hy = _norm_mm(geo, x, norm_mix[layer], mods, 0, w_hy)
            p_rw = _norm_mm(geo, x, norm_mix[layer], mods, 0, w_rw)
            x0, u = _hy_prep(geo, p_hy, hy_short_w[e], hy_short_b[e])
            filt_p = _hyena_filter(lp, hy_w1[e], hy_b1[e], hy_w2[e], hy_b2[e], hy_w3[e], hy_freq[e], dh)
            filt_s = _hyena_filter(ls, hy_w1[e], hy_b1[e], hy_w2[e], hy_b2[e], hy_w3[e], hy_freq[e], dh)
            y_hy = _hyena_long_conv(geo, x0, u, filt_p, filt_s, hy_bias[e])
            mu = _pad_cols(rw_mu[e], 512)
            y_rw, s_ctx = _rwkv_mixer(geo, p_rw, state_rwkv[:, e], mu, rw_w0[e], rw_w2[e], rw_a0[e], rw_a2[e],
                                      rw_g2[e], rw_kk[e], rw_ka[e], rw_rk[e].reshape(dr), rw_ln_w[e], rw_ln_b[e])
            rw_states.append(s_ctx)
            mixed = jnp.concatenate([y_hy, y_rw], axis=-1)
            x = _res_mm(geo, mixed, ev_w_out[e].astype(BF16), x, mods, 2)
        else:
            o = layer // 2
            dk = gla_gk2.shape[3]
            dv = gla_norm.shape[1] * GLA_H
            w_main = od_w_in[o][:, :2 * dk + 2 * dv].astype(BF16)
            w_gk = _pad_cols(od_w_in[o][:, 2 * dk + 2 * dv:], LANES).astype(BF16)
            p = _norm_mm(geo, x, norm_mix[layer], mods, 0, w_main)
            gkd = _norm_mm(geo, x, norm_mix[layer], mods, 0, w_gk)
            y, s_ctx = _gla_mixer(geo, p, gkd, state_gla[:, o], gla_gk2[o], gla_gk_b[o], gla_norm[o], dk, dv)
            gla_states.append(s_ctx)
            x = _res_mm(geo, y, od_w_out[o].astype(BF16), x, mods, 2)
        u_ffn = _norm_mm(geo, x, norm_ffn[layer], mods, 3, ffn_w_up[layer].astype(BF16))
        act = _conv_gate(geo, u_ffn, ffn_conv[layer])
        x = _res_mm(geo, act, ffn_w_down[layer].astype(BF16), x, mods, 5)

    y = _final_norm(geo, x, norm_final)
    y_prompt = y[:geo.np_].reshape(bp, lp, d)
    y_sample = y[geo.np_:].reshape(bs, ls, d)
    return (y_prompt, y_sample, jnp.stack(rw_states, axis=1), jnp.stack(gla_states, axis=1))
```

```python
import functools
import math

import numpy as np
import jax
import jax.numpy as jnp
from jax import lax
from jax.experimental import pallas as pl
from jax.experimental.pallas import tpu as pltpu

F32 = jnp.float32
BF16 = jnp.bfloat16
HI = lax.Precision.HIGHEST

EPS = 1e-6
RW_HEAD = 64
RW_GN_EPS = 64e-5
GLA_H = 4
GLA_GATE_NORM = 16.0
GLA_CHUNK = 64
GRID_W = 64
HY_TARGET = 1e-2
HY_FAST = 0.3
HY_SLOW = 1.5

LANES = 128
SUBLANES = 8
VMEM_LIMIT = 56 * 1024 * 1024
FFT_MINOR = 128


def _cparams(*sem):
    return pltpu.CompilerParams(dimension_semantics=sem, vmem_limit_bytes=VMEM_LIMIT)


def _round_up(n, m):
    return (n + m - 1) // m * m


def _pow2_tile(n, pref):
    t = 1
    while t * 2 <= pref and n % (t * 2) == 0:
        t *= 2
    return t


def _col_tile(n, pref):
    if n % LANES:
        return n
    best = LANES
    for t in range(LANES, min(n, pref) + 1, LANES):
        if n % t == 0:
            best = t
    return best


class _Geo:
    def __init__(self, bp, lp, bs, ls, d):
        self.bp, self.lp, self.bs, self.ls, self.d = bp, lp, bs, ls, d
        self.np_, self.ns = bp * lp, bs * ls
        self.nt = self.np_ + self.ns
        self.tm = _pow2_tile(math.gcd(self.np_, ls), 512)
        assert lp & (lp - 1) == 0 and ls & (ls - 1) == 0, "sequence lengths must be powers of two"
        assert self.np_ % ls == 0, "sample rows must start on a sequence-length boundary"
        assert bs + 1 <= SUBLANES

    def row_tile(self, pref):
        return _pow2_tile(math.gcd(self.np_, self.ls), pref)

    def modrow(self, i, tm):
        r0 = i * tm
        return jnp.where(r0 < self.np_, 0, 1 + (r0 - self.np_) // self.ls)

    def seq_len(self, i, tm):
        return jnp.where(i * tm < self.np_, self.lp, self.ls)

    def mirror_tile(self, i, tm):
        def mirrored(tiles_per_seq):
            return (i // tiles_per_seq) * tiles_per_seq + (tiles_per_seq - 1 - i % tiles_per_seq)
        return jnp.where(i * tm < self.np_, mirrored(self.lp // tm), mirrored(self.ls // tm))

    @staticmethod
    def streams_per_group(dr):
        return LANES // (dr // RW_HEAD)

    def stream_slot(self, i, tm, direction, dr):
        r0 = i * tm
        batch = jnp.where(r0 < self.np_, r0 // self.lp, (r0 - self.np_) // self.ls)
        nbatch = jnp.where(r0 < self.np_, self.bp, self.bs)
        return (direction * nbatch + batch) % self.streams_per_group(dr)


def _adaln_kernel(c_ref, w_ref, b_ref, o_ref):
    c = c_ref[...]
    s = c * jax.nn.sigmoid(c)
    o_ref[0] = jnp.dot(s, w_ref[0], precision=HI, preferred_element_type=F32) + b_ref[0]


def _adaln(cond, w_ada, b_ada):
    depth, d, n = w_ada.shape
    tn = _col_tile(n, 1024)
    return pl.pallas_call(
        _adaln_kernel,
        grid=(depth, n // tn),
        in_specs=[pl.BlockSpec((SUBLANES, d), lambda l, j: (0, 0)),
                  pl.BlockSpec((1, d, tn), lambda l, j: (l, 0, j)),
                  pl.BlockSpec((1, 1, tn), lambda l, j: (l, 0, j))],
        out_specs=pl.BlockSpec((1, SUBLANES, tn), lambda l, j: (l, 0, j)),
        out_shape=jax.ShapeDtypeStruct((depth, SUBLANES, n), F32),
        compiler_params=_cparams("parallel", "arbitrary"),
        name="adaln",
    )(cond, w_ada, b_ada.reshape(depth, 1, n))


def _norm_mm_kernel(x_ref, g_ref, sh_ref, sc_ref, w_ref, o_ref, xn_ref):
    @pl.when(pl.program_id(1) == 0)
    def _():
        x = x_ref[...]
        ms = jnp.mean(x * x, axis=-1, keepdims=True)
        y = x * lax.rsqrt(ms + EPS) * g_ref[...]
        xn_ref[...] = (y * (1.0 + sc_ref[0]) + sh_ref[0]).astype(BF16)

    o_ref[...] = jnp.dot(xn_ref[...], w_ref[...], preferred_element_type=F32).astype(o_ref.dtype)


def _norm_mm(geo, x, g, mods, which_shift, w, out_dtype=F32, tm_pref=512, tn_pref=512):
    nt, d = x.shape
    n = w.shape[1]
    tm = geo.row_tile(tm_pref)
    tn = _col_tile(n, tn_pref)

    def mod_map(which):
        return lambda i, j: (geo.modrow(i, tm) * 6 + which, 0, 0)

    return pl.pallas_call(
        _norm_mm_kernel,
        grid=(nt // tm, n // tn),
        in_specs=[pl.BlockSpec((tm, d), lambda i, j: (i, 0)),
                  pl.BlockSpec((1, d), lambda i, j: (0, 0)),
                  pl.BlockSpec((1, 1, d), mod_map(which_shift)),
                  pl.BlockSpec((1, 1, d), mod_map(which_shift + 1)),
                  pl.BlockSpec((d, tn), lambda i, j: (0, j))],
        out_specs=pl.BlockSpec((tm, tn), lambda i, j: (i, j)),
        out_shape=jax.ShapeDtypeStruct((nt, n), out_dtype),
        scratch_shapes=[pltpu.VMEM((tm, d), BF16)],
        compiler_params=_cparams("parallel", "arbitrary"),
        name="norm_mm",
    )(x, g.reshape(1, d), mods, mods, w)


def _res_mm_kernel(a_ref, w_ref, res_ref, gt_ref, o_ref):
    acc = jnp.dot(a_ref[...], w_ref[...], preferred_element_type=F32)
    o_ref[...] = res_ref[...] + gt_ref[0] * acc


def _res_mm(geo, a, w, res, mods, which_gate, tm_pref=512, tn_pref=512):
    nt, k = a.shape
    n = w.shape[1]
    tm = geo.row_tile(tm_pref)
    tn = _col_tile(n, tn_pref)
    return pl.pallas_call(
        _res_mm_kernel,
        grid=(nt // tm, n // tn),
        in_specs=[pl.BlockSpec((tm, k), lambda i, j: (i, 0)),
                  pl.BlockSpec((k, tn), lambda i, j: (0, j)),
                  pl.BlockSpec((tm, tn), lambda i, j: (i, j)),
                  pl.BlockSpec((1, 1, tn), lambda i, j: (geo.modrow(i, tm) * 6 + which_gate, 0, j))],
        out_specs=pl.BlockSpec((tm, tn), lambda i, j: (i, j)),
        out_shape=jax.ShapeDtypeStruct((nt, n), F32),
        compiler_params=_cparams("parallel", "arbitrary"),
        name="res_mm",
    )(a, w, res, mods)


def _final_norm_kernel(x_ref, g_ref, o_ref):
    x = x_ref[...]
    ms = jnp.mean(x * x, axis=-1, keepdims=True)
    o_ref[...] = x * lax.rsqrt(ms + EPS) * g_ref[...]


def _final_norm(geo, x, g):
    nt, d = x.shape
    tm = geo.tm
    return pl.pallas_call(
        _final_norm_kernel,
        grid=(nt // tm,),
        in_specs=[pl.BlockSpec((tm, d), lambda i: (i, 0)), pl.BlockSpec((1, d), lambda i: (0, 0))],
        out_specs=pl.BlockSpec((tm, d), lambda i: (i, 0)),
        out_shape=jax.ShapeDtypeStruct((nt, d), F32),
        compiler_params=_cparams("parallel"),
        name="final_norm",
    )(x, g.reshape(1, d))


def _seq_pos(geo, i, tm):
    lseq = geo.seq_len(i, tm)
    rid = lax.broadcasted_iota(jnp.int32, (tm, 1), 0) + i * tm
    return rid & (lseq - 1), lseq


def _neighbours(x, prev_row, next_row, tpos, lseq):
    tm = x.shape[0]
    rid = lax.broadcasted_iota(jnp.int32, (tm, 1), 0)
    xm = pltpu.roll(x, 1, axis=0)
    xm = jnp.where(rid == 0, prev_row, xm)
    xm = jnp.where(tpos == 0, 0.0, xm)
    xp = pltpu.roll(x, tm - 1, axis=0)
    xp = jnp.where(rid == tm - 1, next_row, xp)
    xp = jnp.where(tpos == lseq - 1, 0.0, xp)
    return xm, xp


def _halo_specs(nt, tm, width, col):
    r = tm // SUBLANES
    last = nt // SUBLANES - 1
    prev = pl.BlockSpec((SUBLANES, width), lambda i, j: (jnp.maximum(i * r - 1, 0), col(j)))
    nxt = pl.BlockSpec((SUBLANES, width), lambda i, j: (jnp.minimum((i + 1) * r, last), col(j)))
    return prev, nxt


def _hy_prep_kernel(geo, tm, *refs):
    (x0_ref, x0p_ref, x0n_ref, x1_ref, x1p_ref, x1n_ref, x2_ref, x2p_ref, x2n_ref,
     w_ref, b_ref, o0_ref, u_ref) = refs
    i = pl.program_id(0)
    tpos, lseq = _seq_pos(geo, i, tm)

    def conv(x_ref, p_ref, n_ref, g):
        x = x_ref[...]
        xm, xp = _neighbours(x, p_ref[SUBLANES - 1:SUBLANES, :], n_ref[0:1, :], tpos, lseq)
        return xm * w_ref[0, g] + x * w_ref[1, g] + xp * w_ref[2, g] + b_ref[g]

    c0 = conv(x0_ref, x0p_ref, x0n_ref, 0)
    c1 = conv(x1_ref, x1p_ref, x1n_ref, 1)
    c2 = conv(x2_ref, x2p_ref, x2n_ref, 2)
    o0_ref[...] = c0
    u_ref[...] = c1 * c2


def _hy_prep(geo, p_hy, short_w, short_b):
    nt = p_hy.shape[0]
    dh = p_hy.shape[1] // 3
    tm = _pow2_tile(geo.tm, 256)
    tc = _col_tile(dh, 512)
    ncb = dh // tc
    specs = []
    for g in range(3):
        col = (lambda j, g=g: g * ncb + j)
        specs.append(pl.BlockSpec((tm, tc), lambda i, j, col=col: (i, col(j))))
        specs.extend(_halo_specs(nt, tm, tc, col))
    specs.append(pl.BlockSpec((3, 3, 1, tc), lambda i, j: (0, 0, 0, j)))
    specs.append(pl.BlockSpec((3, 1, tc), lambda i, j: (0, 0, j)))
    out_spec = pl.BlockSpec((tm, tc), lambda i, j: (i, j))
    w4 = short_w.reshape(3, 3, 1, dh)
    b3 = short_b.reshape(3, 1, dh)
    return pl.pallas_call(
        functools.partial(_hy_prep_kernel, geo, tm),
        grid=(nt // tm, ncb),
        in_specs=specs,
        out_specs=[out_spec, out_spec],
        out_shape=[jax.ShapeDtypeStruct((nt, dh), F32)] * 2,
        compiler_params=_cparams("parallel", "arbitrary"),
        name="hy_prep",
    )(*([p_hy] * 9), w4, b3)


def _lmm_kernel(has_epi, *refs):
    if has_epi:
        a_ref, x_ref, x0_ref, u_ref, bias_ref, o_ref = refs
    else:
        a_ref, x_ref, o_ref = refs
    acc = jnp.dot(a_ref[0], x_ref[0], precision=HI, preferred_element_type=F32)
    if has_epi:
        acc = x0_ref[0] * (acc + u_ref[0] * bias_ref[...])
    o_ref[0] = acc.astype(o_ref.dtype)


def _lmm(a, x, epi=None, out_dtype=F32):
    gm, m, k = a.shape
    g, _, n = x.shape
    tn = _col_tile(n, 2048)
    xspec = pl.BlockSpec((1, k, tn), lambda gi, j: (gi, 0, j))
    ospec = pl.BlockSpec((1, m, tn), lambda gi, j: (gi, 0, j))
    specs = [pl.BlockSpec((1, m, k), lambda gi, j: (gi % gm, 0, 0)), xspec]
    args = [a, x]
    if epi is not None:
        x0, u, bias = epi
        specs += [ospec, ospec, pl.BlockSpec((1, tn), lambda gi, j: (0, j))]
        args += [x0, u, bias]
    return pl.pallas_call(
        functools.partial(_lmm_kernel, epi is not None),
        grid=(g, n // tn),
        in_specs=specs,
        out_specs=ospec,
        out_shape=jax.ShapeDtypeStruct((g, m, n), out_dtype),
        compiler_params=_cparams("parallel", "arbitrary"),
        name="dft_lmm",
    )(*args)


def _spec_kernel(has_epi, *refs):
    if has_epi:
        mf_ref, h_ref, mi_ref, x_ref, x0_ref, bias_ref, o_ref = refs
    else:
        mf_ref, h_ref, mi_ref, x_ref, o_ref = refs
    x = x_ref[0]
    f = jnp.dot(mf_ref[0], x, precision=HI, preferred_element_type=F32)
    r = f.shape[0] // 2
    fr, fi = f[:r], f[r:]
    hr, hi = h_ref[0, :r], h_ref[0, r:]
    y = jnp.concatenate([fr * hr - fi * hi, fr * hi + fi * hr], axis=0)
    out = jnp.dot(mi_ref[0], y, precision=HI, preferred_element_type=F32)
    if has_epi:
        out = x0_ref[0] * (out + x * bias_ref[...])
    o_ref[0] = out.astype(o_ref.dtype)


def _spectral(mf, h, mi, x, epi=None, out_dtype=F32):
    gm, r2, k = mf.shape
    kout = mi.shape[1]
    g, _, c = x.shape
    tc = _col_tile(c, 512)
    xspec = pl.BlockSpec((1, k, tc), lambda gi, j: (gi, 0, j))
    ospec = pl.BlockSpec((1, kout, tc), lambda gi, j: (gi, 0, j))
    specs = [pl.BlockSpec((1, r2, k), lambda gi, j: (gi % gm, 0, 0)),
             pl.BlockSpec((1, r2, tc), lambda gi, j: (gi % gm, 0, j)),
             pl.BlockSpec((1, kout, r2), lambda gi, j: (gi % gm, 0, 0)),
             xspec]
    args = [mf, h, mi, x]
    if epi is not None:
        x0, bias = epi
        specs += [ospec, pl.BlockSpec((1, tc), lambda gi, j: (0, j))]
        args += [x0, bias]
    return pl.pallas_call(
        functools.partial(_spec_kernel, epi is not None),
        grid=(g, c // tc),
        in_specs=specs,
        out_specs=ospec,
        out_shape=jax.ShapeDtypeStruct((g, kout, c), out_dtype),
        compiler_params=_cparams("parallel", "arbitrary"),
        name="dft_spectral",
    )(*args)


def _dft_tables_direct(l):
    n = 2 * l
    nf = l + 1
    r = _round_up(nf, SUBLANES)
    kk = np.arange(nf)[:, None].astype(np.float64)
    ang_half = 2.0 * np.pi * ((kk * np.arange(l)[None, :]) % n) / n
    ang_full = 2.0 * np.pi * ((kk * np.arange(n)[None, :]) % n) / n

    def fwd(ang):
        m = np.zeros((2 * r, ang.shape[1]))
        m[:nf] = np.cos(ang)
        m[r:r + nf] = -np.sin(ang)
        return m

    coef = np.full((nf,), 2.0 / n)
    coef[0] = coef[-1] = 1.0 / n
    inv = np.zeros((l, 2 * r))
    inv[:, :nf] = (np.cos(ang_half) * coef[:, None]).T
    inv[:, r:r + nf] = (-np.sin(ang_half) * coef[:, None]).T
    f32 = lambda a: jnp.asarray(a[None], F32)
    return f32(fwd(ang_half)), f32(fwd(ang_full)), f32(inv)


def _dft_tables_two_level(l):
    n = 2 * l
    n1 = FFT_MINOR
    n2 = n // n1
    nb = n2 // 2 + 1
    nbp = _round_up(nb, 4)
    k2 = np.arange(nb)[:, None].astype(np.float64)
    ang2 = 2.0 * np.pi * ((k2 * np.arange(n2)[None, :]) % n2) / n2
    ms_full = np.zeros((2 * nbp, n2))
    ms_full[0:2 * nb:2] = np.cos(ang2)
    ms_full[1:2 * nb:2] = -np.sin(ang2)
    ms_half = ms_full[:, :n2 // 2]
    coef = np.full((nb,), 2.0 / n)
    coef[0] = coef[-1] = 1.0 / n
    ms_inv = np.zeros((n2 // 2, 2 * nbp))
    ms_inv[:, 0:2 * nb:2] = (np.cos(ang2[:, :n2 // 2]) * coef[:, None]).T
    ms_inv[:, 1:2 * nb:2] = (-np.sin(ang2[:, :n2 // 2]) * coef[:, None]).T

    a_fwd = np.zeros((nbp, 2 * n1, 2 * n1))
    a_inv = np.zeros((nbp, 2 * n1, 2 * n1))
    i1 = np.arange(n1).astype(np.float64)
    for b in range(nb):
        ph = 2.0 * np.pi * (((np.outer(i1, i1) * n2) + (i1[None, :] * b)) % n) / n
        wr, wi = np.cos(ph), -np.sin(ph)
        a_fwd[b] = np.block([[wr, -wi], [wi, wr]])
        ph = 2.0 * np.pi * (((np.outer(i1, i1) * n2) + (i1[:, None] * b)) % n) / n
        pr, pi_ = np.cos(ph), np.sin(ph)
        a_inv[b] = np.block([[pr, -pi_], [pi_, pr]])
    f32 = lambda a: jnp.asarray(a, F32)
    return (f32(ms_half[None]), f32(ms_full[None]), f32(ms_inv[None]), f32(a_fwd), f32(a_inv), n1, n2, nbp)


def _hyena_filter(l, w1, b1, w2, b2, w3, freq, dh):
    pos = jnp.arange(l, dtype=F32)[:, None]
    t = pos / max(l - 1, 1)
    emb = w1.shape[0]
    bands = (emb - 1) // 2
    fb = jnp.linspace(1e-4, bands - 1, bands, dtype=F32)
    ang = (2.0 * math.pi / l) * pos * fb
    z = jnp.concatenate([t, jnp.cos(ang), -jnp.sin(ang)], axis=-1)
    h = jnp.sin(freq[0] * (jnp.dot(z, w1, precision=HI) + b1))
    h = jnp.sin(freq[1] * (jnp.dot(h, w2, precision=HI) + b2))
    h = jnp.dot(h, w3, precision=HI).reshape(l, 2, dh)
    deltas = jnp.abs(jnp.linspace(math.log(HY_TARGET) / HY_SLOW, math.log(HY_TARGET) / HY_FAST, dh, dtype=F32))
    h = h * jnp.exp(-t[:, :, None] * deltas)
    filt = jnp.concatenate([h[:, 0], jnp.zeros((1, dh), F32), jnp.flip(h[1:, 1], axis=0)], axis=0)
    return filt / jnp.sum(jnp.abs(filt), axis=0, keepdims=True)


def _hyena_long_conv(geo, x0, u, filt_p, filt_s, bias):
    dh = u.shape[1]
    bias2 = bias.reshape(1, dh)
    mf_half, mf_full, mi = _dft_tables_direct(geo.lp)
    h_p = _lmm(mf_full, filt_p[None])
    up = u[:geo.np_].reshape(geo.bp, geo.lp, dh)
    x0p = x0[:geo.np_].reshape(geo.bp, geo.lp, dh)
    y_p = _spectral(mf_half, h_p, mi, up, epi=(x0p, bias2), out_dtype=BF16)
    ms_half, ms_full, ms_inv, a_fwd, a_inv, n1, n2, nbp = _dft_tables_two_level(geo.ls)
    hs = _lmm(ms_full, filt_s.reshape(1, n2, n1 * dh))
    hs = _lmm(a_fwd, hs.reshape(nbp, 2 * n1, dh))
    us = u[geo.np_:].reshape(geo.bs, n2 // 2, n1 * dh)
    x0s = x0[geo.np_:].reshape(geo.bs, n2 // 2, n1 * dh)
    xs = _lmm(ms_half, us)
    zs = _spectral(a_fwd, hs, a_inv, xs.reshape(geo.bs * nbp, 2 * n1, dh))
    zs = zs.reshape(geo.bs, 2 * nbp, n1 * dh)
    y_s = _lmm(ms_inv, zs, epi=(x0s, us, jnp.tile(bias2, (1, n1))), out_dtype=BF16)
    return jnp.concatenate([y_p.reshape(geo.np_, dh), y_s.reshape(geo.ns, dh)], axis=0)


def _seg_sum(x, e_ref, et_ref):
    s = jnp.dot(x, e_ref[...], precision=HI, preferred_element_type=F32)
    return jnp.dot(s, et_ref[...], precision=HI, preferred_element_type=F32)


def _rw_prep_kernel(geo, tm, dr, *refs):
    (x_ref, xp_ref, xn_ref, mu_ref, w0_ref, w2_ref, a0_ref, a2_ref, g2_ref, kk_ref, ka_ref, rk_ref,
     e_ref, et_ref,
     rf_o, nkkf_o, vf_o, wf_o, kxf_o, bf_o, rb_o, nkkb_o, vb_o, wb_o, kxb_o, bb_o, g_o, bon_o) = refs
    i = pl.program_id(0)
    tpos, lseq = _seq_pos(geo, i, tm)
    x = x_ref[...]
    xm, xp = _neighbours(x, xp_ref[SUBLANES - 1:SUBLANES, :], xn_ref[0:1, :], tpos, lseq)
    x = x + mu_ref[...] * (0.5 * (xm + xp) - x)
    r = x[:, 0:dr]
    k = x[:, dr:2 * dr]
    v = x[:, 2 * dr:3 * dr]
    wd = x[:, 3 * dr:3 * dr + LANES]
    ad = x[:, 3 * dr + LANES:3 * dr + 2 * LANES]
    gd = x[:, 3 * dr + 2 * LANES:3 * dr + 3 * LANES]
    wl = w0_ref[...] + jnp.dot(jnp.tanh(wd), w2_ref[...], precision=HI, preferred_element_type=F32)
    w_log = -jax.nn.softplus(-wl) - 0.5
    decay = jnp.exp(-jnp.exp(w_log))
    a = jax.nn.sigmoid(a0_ref[...] + jnp.dot(ad, a2_ref[...], precision=HI, preferred_element_type=F32))
    g_o[...] = jnp.dot(jax.nn.sigmoid(gd), g2_ref[...], precision=HI, preferred_element_type=F32)
    kk = k * kk_ref[...]
    kk = kk * lax.rsqrt(_seg_sum(kk * kk, e_ref, et_ref) + 1e-12)
    a_0, a_1 = a[:, :dr], a[:, dr:]
    kx0 = k * (1.0 + (a_0 - 1.0) * ka_ref[...])
    kx1 = k * (1.0 + (a_1 - 1.0) * ka_ref[...])
    bon_o[...] = _seg_sum(r * (kx0 + kx1) * rk_ref[...], e_ref, et_ref) * v

    rr = lax.broadcasted_iota(jnp.int32, (tm, tm), 0)
    cc = lax.broadcasted_iota(jnp.int32, (tm, tm), 1)
    flip = (rr + cc == tm - 1).astype(F32)
    slab = LANES // geo.streams_per_group(dr)

    def emit(o_ref, val, direction):
        if direction:
            val = jnp.dot(flip, val, precision=HI, preferred_element_type=F32)
        shift = geo.stream_slot(i, tm, direction, dr) * slab
        for q in range(dr // LANES):
            o_ref[:, q * LANES:(q + 1) * LANES] = pltpu.roll(val[:, q * LANES:(q + 1) * LANES], shift, axis=1)

    nkk = -kk
    for o_f, o_b, val_f, val_b in ((rf_o, rb_o, r, r), (nkkf_o, nkkb_o, nkk, nkk), (vf_o, vb_o, v, v),
                                   (wf_o, wb_o, decay[:, :dr], decay[:, dr:]), (kxf_o, kxb_o, kx0, kx1),
                                   (bf_o, bb_o, kk * a_0, kk * a_1)):
        emit(o_f, val_f, 0)
        emit(o_b, val_b, 1)


def _blockdiag2(w):
    _, r, c = w.shape
    z = jnp.zeros((r, c), w.dtype)
    return jnp.concatenate([jnp.concatenate([w[0], z], axis=1), jnp.concatenate([z, w[1]], axis=1)], axis=0)


def _head_minor_perm(dr):
    return np.arange(dr).reshape(dr // RW_HEAD, RW_HEAD).T.reshape(-1)


def _head_indicator(dr):
    h = dr // RW_HEAD
    hp = _round_up(h, LANES)
    e = np.zeros((dr, hp), np.float32)
    e[np.arange(dr), np.arange(dr) % h] = 1.0
    return jnp.asarray(e), jnp.asarray(e.T.copy())


def _rw_prep(geo, p_rw, mu, w0, w2, a0, a2, g2, k_k, k_a, r_k):
    nt, width = p_rw.shape
    dr = k_k.shape[0]
    assert 2 * w2.shape[1] == LANES and 2 * a2.shape[1] == LANES and g2.shape[0] == LANES
    tm = _pow2_tile(min(geo.tm, geo.lp), 128)
    full = lambda shape: pl.BlockSpec(shape, lambda i, j: tuple(0 for _ in shape))
    prev, nxt = _halo_specs(nt, tm, width, lambda j: 0)
    e, et = _head_indicator(dr)
    consts = [mu.reshape(1, width), w0.reshape(1, 2 * dr), _blockdiag2(w2), a0.reshape(1, 2 * dr),
              _blockdiag2(a2), g2, k_k.reshape(1, dr), k_a.reshape(1, dr), r_k.reshape(1, dr), e, et]
    ospec = pl.BlockSpec((tm, dr), lambda i, j: (i, 0))
    mspec = pl.BlockSpec((tm, dr), lambda i, j: (geo.mirror_tile(i, tm), 0))
    return pl.pallas_call(
        functools.partial(_rw_prep_kernel, geo, tm, dr),
        grid=(nt // tm, 1),
        in_specs=[pl.BlockSpec((tm, width), lambda i, j: (i, 0)), prev, nxt] + [full(c.shape) for c in consts],
        out_specs=[ospec] * 6 + [mspec] * 6 + [ospec] * 2,
        out_shape=[jax.ShapeDtypeStruct((nt, dr), F32)] * 14,
        compiler_params=_cparams("parallel", "arbitrary"),
        name="rw_prep",
    )(p_rw, p_rw, p_rw, *consts)


def _rw_scan_kernel(tc, r_ref, nkk_ref, v_ref, w_ref, kx_ref, b_ref, s0_ref, y_ref, so_ref, s_ref):
    t_chunk = pl.program_id(1)

    @pl.when(t_chunk == 0)
    def _():
        s_ref[...] = s0_ref[...]

    nacc = 4

    def step(t, carry):
        acc = [jnp.zeros((RW_HEAD, LANES), F32) for _ in range(nacc)]
        for k in range(RW_HEAD):
            acc[k % nacc] = acc[k % nacc] + s_ref[k] * nkk_ref[t, k:k + 1, :]
        sa = (acc[0] + acc[1]) + (acc[2] + acc[3])
        vv = v_ref[t]
        acc = [jnp.zeros((RW_HEAD, LANES), F32) for _ in range(nacc)]
        for k in range(RW_HEAD):
            new = s_ref[k] * w_ref[t, k:k + 1, :] + sa * b_ref[t, k:k + 1, :] + vv * kx_ref[t, k:k + 1, :]
            s_ref[k] = new
            acc[k % nacc] = acc[k % nacc] + new * r_ref[t, k:k + 1, :]
        y_ref[t] = (acc[0] + acc[1]) + (acc[2] + acc[3])
        return carry

    lax.fori_loop(0, tc, step, 0)

    @pl.when(t_chunk == pl.num_programs(1) - 1)
    def _():
        so_ref[...] = s_ref[...]


def _rw_scan(r, nkk, v, w, kx, b, s0):
    t_len, n, lanes = r.shape
    tc = _pow2_tile(t_len, 32)
    xspec = pl.BlockSpec((tc, n, LANES), lambda g, t: (t, 0, g))
    sspec = pl.BlockSpec((n, n, LANES), lambda g, t: (0, 0, g))
    return pl.pallas_call(
        functools.partial(_rw_scan_kernel, tc),
        grid=(lanes // LANES, t_len // tc),
        in_specs=[xspec] * 6 + [sspec],
        out_specs=[xspec, sspec],
        out_shape=[jax.ShapeDtypeStruct((t_len, n, lanes), F32), jax.ShapeDtypeStruct((n, n, lanes), F32)],
        scratch_shapes=[pltpu.VMEM((n, n, LANES), F32)],
        compiler_params=_cparams("parallel", "arbitrary"),
        name="rw_scan",
    )(r, nkk, v, w, kx, b, s0)


def _merge_slabs(pieces, slab, offset):
    n = len(pieces)
    lane_slab = lax.broadcasted_iota(jnp.int32, pieces[0].shape, 1) // slab
    acc = pieces[n - 1]
    for p in range(n - 1):
        acc = jnp.where(lane_slab == (offset + p) % n, pieces[p], acc)
    return acc


def _to_streams_kernel(nsrc, ntens, slab, *refs):
    srcs, outs = refs[:ntens * nsrc], refs[ntens * nsrc:]
    for ti in range(ntens):
        tiles = [srcs[ti * nsrc + s][...] for s in range(nsrc)]
        for j in range(SUBLANES):
            merged = _merge_slabs(tiles, slab, j)
            shift = ((nsrc - j) % nsrc) * slab
            outs[ti][:, j, :] = pltpu.roll(merged, shift, axis=1) if shift else merged


def _to_streams(fwd, bwd, row0, nbatch, l, direction):
    ntens = len(fwd)
    dr = fwd[0].shape[1]
    nsrc = _Geo.streams_per_group(dr)
    assert nsrc == SUBLANES, "the stream layout assumes 16 heads (eight channels per 128-lane tile)"
    slab = LANES // nsrc
    tt = _pow2_tile(l, 128)
    if direction is None:
        assert 2 * nbatch == nsrc
        ngroups = 1
        source = lambda g, s: (s // nbatch, s % nbatch)
    else:
        assert nbatch % nsrc == 0
        ngroups = nbatch // nsrc
        source = lambda g, s: (direction, g * nsrc + s)
    specs, args = [], []
    for ti in range(ntens):
        for s in range(nsrc):
            d = source(0, s)[0]
            specs.append(pl.BlockSpec(
                (tt, LANES), lambda g, i, q, s=s: ((row0 + source(g, s)[1] * l) // tt + i, q)))
            args.append((bwd if d else fwd)[ti])
    ospec = pl.BlockSpec((tt, SUBLANES, LANES), lambda g, i, q: (i, q, g))
    return pl.pallas_call(
        functools.partial(_to_streams_kernel, nsrc, ntens, slab),
        grid=(ngroups, l // tt, dr // LANES),
        in_specs=specs,
        out_specs=[ospec] * ntens,
        out_shape=[jax.ShapeDtypeStruct((l, RW_HEAD, ngroups * LANES), F32)] * ntens,
        compiler_params=_cparams("parallel", "parallel", "arbitrary"),
        name="to_streams",
    )(*args)


def _from_streams_kernel(nsrc, slab, y_ref, o_ref):
    rolled = [y_ref[:, j, :] if j == 0 else pltpu.roll(y_ref[:, j, :], j * slab, axis=1) for j in range(SUBLANES)]
    for s in range(nsrc):
        merged = _merge_slabs(rolled, slab, s)
        shift = ((nsrc - s) % nsrc) * slab
        o_ref[s] = pltpu.roll(merged, shift, axis=1) if shift else merged


def _from_streams(y, dr):
    l, _, lanes = y.shape
    nsrc = _Geo.streams_per_group(dr)
    slab = LANES // nsrc
    tt = _pow2_tile(l, 128)
    ngroups = lanes // LANES
    return pl.pallas_call(
        functools.partial(_from_streams_kernel, nsrc, slab),
        grid=(ngroups, l // tt, dr // LANES),
        in_specs=[pl.BlockSpec((tt, SUBLANES, LANES), lambda g, i, q: (i, q, g))],
        out_specs=pl.BlockSpec((nsrc, tt, LANES), lambda g, i, q: (g, i, q)),
        out_shape=jax.ShapeDtypeStruct((ngroups * nsrc, l, dr), F32),
        compiler_params=_cparams("parallel", "parallel", "arbitrary"),
        name="from_streams",
    )(y)


def _rw_post_kernel(yf_ref, yb_ref, bon_ref, g_ref, lnw_ref, lnb_ref, e_ref, et_ref, o_ref):
    tm = yf_ref.shape[0]
    rr = lax.broadcasted_iota(jnp.int32, (tm, tm), 0)
    cc = lax.broadcasted_iota(jnp.int32, (tm, tm), 1)
    flip = (rr + cc == tm - 1).astype(F32)
    y = yf_ref[...] + jnp.dot(flip, yb_ref[...], precision=HI, preferred_element_type=F32)
    inv_n = 1.0 / RW_HEAD
    mean = _seg_sum(y, e_ref, et_ref) * inv_n
    yc = y - mean
    var = _seg_sum(yc * yc, e_ref, et_ref) * inv_n
    yn = yc * lax.rsqrt(var + RW_GN_EPS) * lnw_ref[...] + lnb_ref[...]
    o_ref[...] = ((yn + bon_ref[...]) * g_ref[...]).astype(o_ref.dtype)


def _rw_post(geo, yf, yb, bonus, g, ln_w, ln_b):
    nt, dr = yf.shape
    tm = _pow2_tile(min(geo.tm, geo.lp), 256)
    e, et = _head_indicator(dr)
    tspec = pl.BlockSpec((tm, dr), lambda i: (i, 0))
    mspec = pl.BlockSpec((tm, dr), lambda i: (geo.mirror_tile(i, tm), 0))
    full = lambda a: pl.BlockSpec(a.shape, lambda i: (0, 0))
    consts = [ln_w.reshape(1, dr), ln_b.reshape(1, dr), e, et]
    return pl.pallas_call(
        _rw_post_kernel,
        grid=(nt // tm,),
        in_specs=[tspec, mspec, tspec, tspec] + [full(c) for c in consts],
        out_specs=tspec,
        out_shape=jax.ShapeDtypeStruct((nt, dr), BF16),
        compiler_params=_cparams("parallel"),
        name="rw_post",
    )(yf, yb, bonus, g, *consts)


def _rwkv_mixer(geo, p_rw, s0_sample, mu, w0, w2, a0, a2, g2, k_k, k_a, r_k, ln_w, ln_b):
    dr = k_k.shape[0]
    h = dr // RW_HEAD
    outs = _rw_prep(geo, p_rw, mu, w0, w2, a0, a2, g2, k_k, k_a, r_k)
    fwd, bwd, (g, bonus) = outs[0:6], outs[6:12], outs[12:14]
    s0 = s0_sample.transpose(4, 3, 1, 0, 2).reshape(RW_HEAD, RW_HEAD, 2 * geo.bs * h)
    y_s, _ = _rw_scan(*_to_streams(fwd, bwd, geo.np_, geo.bs, geo.ls, None), s0)
    y_s = _from_streams(y_s, dr)
    y_p, s_p = [], []
    for direction in range(2):
        zero = jnp.zeros((RW_HEAD, RW_HEAD, geo.bp * h), F32)
        y_d, s_d = _rw_scan(*_to_streams(fwd, bwd, 0, geo.bp, geo.lp, direction), zero)
        y_p.append(_from_streams(y_d, dr))
        s_p.append(s_d.reshape(RW_HEAD, RW_HEAD, geo.bp, h))
    states = jnp.stack(s_p).transpose(3, 0, 4, 2, 1)
    yf = jnp.concatenate([y_p[0].reshape(geo.np_, dr), y_s[:geo.bs].reshape(geo.ns, dr)], axis=0)
    yb = jnp.concatenate([y_p[1].reshape(geo.np_, dr), y_s[geo.bs:].reshape(geo.ns, dr)], axis=0)
    return _rw_post(geo, yf, yb, bonus, g, ln_w, ln_b), states


def _gla_dir(q, k, v, gkd, gk2, gkb, s, reverse):
    c = q.shape[0]
    z = jnp.dot(gkd, gk2, precision=HI, preferred_element_type=F32) + gkb
    lg = (jnp.minimum(z, 0.0) - jnp.log1p(jnp.exp(-jnp.abs(z)))) * (1.0 / GLA_GATE_NORM)
    row = lax.broadcasted_iota(jnp.int32, (c, c), 0)
    col = lax.broadcasted_iota(jnp.int32, (c, c), 1)
    tri = (col >= row) if reverse else (col <= row)
    b = jnp.dot(tri.astype(F32), lg, precision=HI, preferred_element_type=F32)
    i_ref = c // 2 if reverse else c // 2 - 1
    i_last = 0 if reverse else c - 1
    b_ref = b[i_ref:i_ref + 1, :]
    b_last = b[i_last:i_last + 1, :]
    vb = v.astype(BF16)
    scores = lax.dot_general((q * jnp.exp(b - b_ref)).astype(BF16), (k * jnp.exp(b_ref - b)).astype(BF16),
                             (((1,), (1,)), ((), ())), preferred_element_type=F32)
    scores = jnp.where(tri, scores, 0.0)
    o = jnp.dot(scores.astype(BF16), vb, preferred_element_type=F32)
    o = o + jnp.dot((q * jnp.exp(b)).astype(BF16), s.astype(BF16), preferred_element_type=F32)
    bt = b.T
    bt_last = bt[:, i_last:i_last + 1]
    k_dec_t = (k.T * jnp.exp(bt_last - bt)).astype(BF16)
    s_new = s * jnp.exp(bt_last) + jnp.dot(k_dec_t, vb, preferred_element_type=F32)
    return o, s_new


def _gla_kernel(scale, *refs):
    (qf_ref, kf_ref, vf_ref, df_ref, qb_ref, kb_ref, vb_ref, db_ref, gk2_ref, gkb_ref, s0_ref,
     of_ref, ob_ref, so_ref, sf, sb) = refs
    ci = pl.program_id(2)

    @pl.when(ci == 0)
    def _():
        sf[...] = s0_ref[0, 0, 0]
        sb[...] = s0_ref[0, 1, 0]

    rank = gk2_ref.shape[1]
    o, s_new = _gla_dir(qf_ref[...] * scale, kf_ref[...], vf_ref[...], df_ref[:, 0:rank],
                        gk2_ref[0], gkb_ref[0], sf[...], False)
    of_ref[...] = o
    sf[...] = s_new
    o, s_new = _gla_dir(qb_ref[...] * scale, kb_ref[...], vb_ref[...], db_ref[:, rank:2 * rank],
                        gk2_ref[1], gkb_ref[1], sb[...], True)
    ob_ref[...] = o
    sb[...] = s_new

    @pl.when(ci == pl.num_programs(2) - 1)
    def _():
        so_ref[0, 0, 0] = sf[...]
        so_ref[0, 1, 0] = sb[...]


def _gla_scan(p, gkd, row0, b, l, dk, dv, gk2, gk_b, s0):
    hk, hv = dk // GLA_H, dv // GLA_H
    c = GLA_CHUNK
    n = l // c
    blk0 = row0 // c
    rank = gk2.shape[1]

    def tok(bi, ci, rev):
        return blk0 + bi * n + (n - 1 - ci if rev else ci)

    def specs(rev):
        return [pl.BlockSpec((c, hk), lambda bi, hi, ci: (tok(bi, ci, rev), hi)),
                pl.BlockSpec((c, hk), lambda bi, hi, ci: (tok(bi, ci, rev), GLA_H + hi)),
                pl.BlockSpec((c, hv), lambda bi, hi, ci: (tok(bi, ci, rev), 2 * dk // hv + hi)),
                pl.BlockSpec((c, LANES), lambda bi, hi, ci: (tok(bi, ci, rev), 0))]

    def ospec(rev):
        return pl.BlockSpec((c, hv), lambda bi, hi, ci: (bi * n + (n - 1 - ci if rev else ci), hi))

    sspec = pl.BlockSpec((1, 2, 1, hk, hv), lambda bi, hi, ci: (bi, 0, hi, 0, 0))
    return pl.pallas_call(
        functools.partial(_gla_kernel, float(hk) ** -0.5),
        grid=(b, GLA_H, n),
        in_specs=specs(False) + specs(True) + [
            pl.BlockSpec((2, rank, hk), lambda bi, hi, ci: (0, 0, hi)),
            pl.BlockSpec((2, 1, hk), lambda bi, hi, ci: (0, 0, hi)),
            sspec],
        out_specs=[ospec(False), ospec(True), sspec],
        out_shape=[jax.ShapeDtypeStruct((b * l, dv), F32), jax.ShapeDtypeStruct((b * l, dv), F32),
                   jax.ShapeDtypeStruct((b, 2, GLA_H, hk, hv), F32)],
        scratch_shapes=[pltpu.VMEM((hk, hv), F32), pltpu.VMEM((hk, hv), F32)],
        compiler_params=_cparams("parallel", "parallel", "arbitrary"),
        name="gla_scan",
    )(p, p, p, gkd, p, p, p, gkd, gk2, gk_b.reshape(2, 1, dk), s0)


def _gla_post_kernel(hv, of_ref, ob_ref, g_ref, nw_ref, o_ref):
    o = of_ref[...] + ob_ref[...]
    g = g_ref[...]
    for h in range(GLA_H):
        oh = o[:, h * hv:(h + 1) * hv]
        gh = g[:, h * hv:(h + 1) * hv]
        oh = oh * lax.rsqrt(jnp.mean(oh * oh, axis=-1, keepdims=True) + EPS) * nw_ref[...]
        o_ref[:, h * hv:(h + 1) * hv] = (oh * (gh * jax.nn.sigmoid(gh))).astype(o_ref.dtype)


def _gla_post(geo, o_f, o_b, p, dk, dv, norm_w):
    nt = o_f.shape[0]
    tm = _pow2_tile(geo.tm, 256)
    hv = dv // GLA_H
    tspec = pl.BlockSpec((tm, dv), lambda i: (i, 0))
    return pl.pallas_call(
        functools.partial(_gla_post_kernel, hv),
        grid=(nt // tm,),
        in_specs=[tspec, tspec, pl.BlockSpec((tm, dv), lambda i: (i, 2 * dk // dv + 1)),
                  pl.BlockSpec((1, hv), lambda i: (0, 0))],
        out_specs=tspec,
        out_shape=jax.ShapeDtypeStruct((nt, dv), BF16),
        compiler_params=_cparams("parallel"),
        name="gla_post",
    )(o_f, o_b, p, norm_w.reshape(1, hv))


def _gla_mixer(geo, p, gkd, s0_sample, gk2, gk_b, norm_w, dk, dv):
    hk, hv = dk // GLA_H, dv // GLA_H
    s0_p = jnp.zeros((geo.bp, 2, GLA_H, hk, hv), F32)
    of_p, ob_p, s_p = _gla_scan(p, gkd, 0, geo.bp, geo.lp, dk, dv, gk2, gk_b, s0_p)
    of_s, ob_s, _ = _gla_scan(p, gkd, geo.np_, geo.bs, geo.ls, dk, dv, gk2, gk_b, s0_sample)
    o_f = jnp.concatenate([of_p, of_s], axis=0)
    o_b = jnp.concatenate([ob_p, ob_s], axis=0)
    return _gla_post(geo, o_f, o_b, p, dk, dv, norm_w), s_p


def _conv_gate_kernel(geo, tm, g_ref, gp_ref, gn_ref, v_ref, w_ref, o_ref):
    i = pl.program_id(0)
    is_prompt = i * tm < geo.np_
    halo = GRID_W
    n_ext = tm + 2 * halo

    def conv(lseq, ncol):
        ext = jnp.concatenate([gp_ref[...], g_ref[...], gn_ref[...]], axis=0)
        epos = (lax.broadcasted_iota(jnp.int32, (n_ext, 1), 0) + (i * tm - halo)) & (lseq - 1)
        scol = epos & (ncol - 1)
        taps = {-1: pltpu.roll(jnp.where(scol == ncol - 1, 0.0, ext), 1, axis=0),
                0: ext,
                1: pltpu.roll(jnp.where(scol == 0, 0.0, ext), n_ext - 1, axis=0)}

        def row_sum(di):
            start = halo + di * GRID_W
            return sum(taps[dj][start:start + tm, :] * w_ref[di + 1, dj + 1] for dj in (-1, 0, 1))

        acc = row_sum(0)
        nrow = lseq // ncol
        if nrow > 1:
            assert ncol == GRID_W
            tpos = (lax.broadcasted_iota(jnp.int32, (tm, 1), 0) + i * tm) & (lseq - 1)
            row = tpos >> int(math.log2(ncol))
            acc = acc + jnp.where(row >= 1, row_sum(-1), 0.0) + jnp.where(row <= nrow - 2, row_sum(1), 0.0)
        o_ref[...] = (acc * jax.nn.sigmoid(acc) * v_ref[...]).astype(o_ref.dtype)

    @pl.when(is_prompt)
    def _():
        conv(geo.lp, geo.lp)

    @pl.when(jnp.logical_not(is_prompt))
    def _():
        conv(geo.ls, GRID_W)


def _conv_gate(geo, u, conv_w):
    nt = u.shape[0]
    f = u.shape[1] // 2
    tm = _pow2_tile(geo.tm, 256)
    assert tm % GRID_W == 0
    tc = _col_tile(f, 512)
    ncb = f // tc
    r = tm // GRID_W
    last = nt // GRID_W - 1
    return pl.pallas_call(
        functools.partial(_conv_gate_kernel, geo, tm),
        grid=(nt // tm, ncb),
        in_specs=[pl.BlockSpec((tm, tc), lambda i, j: (i, j)),
                  pl.BlockSpec((GRID_W, tc), lambda i, j: (jnp.maximum(i * r - 1, 0), j)),
                  pl.BlockSpec((GRID_W, tc), lambda i, j: (jnp.minimum((i + 1) * r, last), j)),
                  pl.BlockSpec((tm, tc), lambda i, j: (i, ncb + j)),
                  pl.BlockSpec((3, 3, 1, tc), lambda i, j: (0, 0, 0, j))],
        out_specs=pl.BlockSpec((tm, tc), lambda i, j: (i, j)),
        out_shape=jax.ShapeDtypeStruct((nt, f), BF16),
        compiler_params=_cparams("parallel", "arbitrary"),
        name="conv_gate",
    )(u, u, u, u, conv_w.reshape(3, 3, 1, f))


def _pad_cols(w, mult):
    n = w.shape[-1]
    pad = _round_up(n, mult) - n
    return jnp.pad(w, [(0, 0)] * (w.ndim - 1) + [(0, pad)]) if pad else w


def kernel(x_prompt, x_sample, state_rwkv, state_gla, c, c_ctx, w_ada, b_ada, norm_mix, norm_ffn, ffn_w_up, ffn_conv, ffn_w_down, norm_final, ev_w_in, ev_w_out, hy_short_w, hy_short_b, hy_w1, hy_b1, hy_w2, hy_b2, hy_w3, hy_freq, hy_bias, rw_mu, rw_w0, rw_w2, rw_a0, rw_a2, rw_g2, rw_kk, rw_ka, rw_rk, rw_ln_w, rw_ln_b, od_w_in, od_w_out, gla_gk2, gla_gk_b, gla_norm):
    bp, lp, d = x_prompt.shape
    bs, ls, _ = x_sample.shape
    geo = _Geo(bp, lp, bs, ls, d)
    depth = w_ada.shape[0]
    x = jnp.concatenate([x_prompt.reshape(geo.np_, d), x_sample.reshape(geo.ns, d)], axis=0)

    cond = jnp.concatenate([c_ctx[None, :], c], axis=0)
    cond = jnp.pad(cond, ((0, SUBLANES - cond.shape[0]), (0, 0)))
    mods_all = _adaln(cond, w_ada, b_ada)

    rw_states, gla_states = [], []
    for layer in range(depth):
        mods = mods_all[layer].reshape(SUBLANES * 6, 1, d)
        if layer % 2 == 0:
            e = layer // 2
            dh = hy_bias.shape[1]
            dr = rw_kk.shape[1]
            perm = _head_minor_perm(dr)
            perm3 = np.concatenate([perm, dr + perm, 2 * dr + perm, np.arange(3 * dr, ev_w_in.shape[2] - 3 * dh)])
            w_hy = ev_w_in[e][:, :3 * dh].astype(BF16)
            w_rw = _pad_cols(ev_w_in[e][:, 3 * dh:][:, perm3], 512).astype(BF16)
            p_hy = _norm_mm(geo, x, norm_mix[layer], mods, 0, w_hy, tm_pref=1024, tn_pref=1024)
            p_rw = _norm_mm(geo, x, norm_mix[layer], mods, 0, w_rw)
            x0, u = _hy_prep(geo, p_hy, hy_short_w[e], hy_short_b[e])
            filt_p = _hyena_filter(lp, hy_w1[e], hy_b1[e], hy_w2[e], hy_b2[e], hy_w3[e], hy_freq[e], dh)
            filt_s = _hyena_filter(ls, hy_w1[e], hy_b1[e], hy_w2[e], hy_b2[e], hy_w3[e], hy_freq[e], dh)
            y_hy = _hyena_long_conv(geo, x0, u, filt_p, filt_s, hy_bias[e])
            mu = _pad_cols(rw_mu[e][perm3], 512)
            y_rw, s_ctx = _rwkv_mixer(geo, p_rw, state_rwkv[:, e], mu, rw_w0[e][:, perm], rw_w2[e][:, :, perm],
                                      rw_a0[e][:, perm], rw_a2[e][:, :, perm], rw_g2[e][:, perm], rw_kk[e][perm],
                                      rw_ka[e][perm], rw_rk[e].reshape(dr)[perm], rw_ln_w[e][perm], rw_ln_b[e][perm])
            rw_states.append(s_ctx)
            mixed = jnp.concatenate([y_hy, y_rw], axis=-1)
            w_out = jnp.concatenate([ev_w_out[e][:dh], ev_w_out[e][dh:][perm]], axis=0).astype(BF16)
            x = _res_mm(geo, mixed, w_out, x, mods, 2, tm_pref=1024, tn_pref=512)
        else:
            o = layer // 2
            dk = gla_gk2.shape[3]
            dv = gla_norm.shape[1] * GLA_H
            w_main = od_w_in[o][:, :2 * dk + 2 * dv].astype(BF16)
            w_gk = _pad_cols(od_w_in[o][:, 2 * dk + 2 * dv:], LANES).astype(BF16)
            p = _norm_mm(geo, x, norm_mix[layer], mods, 0, w_main, tm_pref=1024, tn_pref=512)
            gkd = _norm_mm(geo, x, norm_mix[layer], mods, 0, w_gk)
            y, s_ctx = _gla_mixer(geo, p, gkd, state_gla[:, o], gla_gk2[o], gla_gk_b[o], gla_norm[o], dk, dv)
            gla_states.append(s_ctx)
            x = _res_mm(geo, y, od_w_out[o].astype(BF16), x, mods, 2, tm_pref=512, tn_pref=1024)
        tiles = (1024, 512) if layer % 2 == 0 else (512, 1024)
        u_ffn = _norm_mm(geo, x, norm_ffn[layer], mods, 3, ffn_w_up[layer].astype(BF16),
                         tm_pref=tiles[0], tn_pref=tiles[1])
        act = _conv_gate(geo, u_ffn, ffn_conv[layer])
        x = _res_mm(geo, act, ffn_w_down[layer].astype(BF16), x, mods, 5, tm_pref=tiles[0], tn_pref=512)

    y = _final_norm(geo, x, norm_final)
    y_prompt = y[:geo.np_].reshape(bp, lp, d)
    y_sample = y[geo.np_:].reshape(bs, ls, d)
    return (y_prompt, y_sample, jnp.stack(rw_states, axis=1), jnp.stack(gla_states, axis=1))
```

```python
import functools
import math

import numpy as np
import jax
import jax.numpy as jnp
from jax import lax
from jax.experimental import pallas as pl
from jax.experimental.pallas import tpu as pltpu

F32 = jnp.float32
BF16 = jnp.bfloat16
HI = lax.Precision.HIGHEST

EPS = 1e-6
RW_HEAD = 64
RW_GN_EPS = 64e-5
GLA_H = 4
GLA_GATE_NORM = 16.0
GLA_CHUNK = 64
GRID_W = 64
HY_TARGET = 1e-2
HY_FAST = 0.3
HY_SLOW = 1.5

LANES = 128
SUBLANES = 8
VMEM_LIMIT = 56 * 1024 * 1024
FFT_MINOR = 128


def _cparams(*sem):
    return pltpu.CompilerParams(dimension_semantics=sem, vmem_limit_bytes=VMEM_LIMIT)


def _round_up(n, m):
    return (n + m - 1) // m * m


def _pow2_tile(n, pref):
    t = 1
    while t * 2 <= pref and n % (t * 2) == 0:
        t *= 2
    return t


def _col_tile(n, pref):
    if n % LANES:
        return n
    best = LANES
    for t in range(LANES, min(n, pref) + 1, LANES):
        if n % t == 0:
            best = t
    return best


class _Geo:
    def __init__(self, bp, lp, bs, ls, d):
        self.bp, self.lp, self.bs, self.ls, self.d = bp, lp, bs, ls, d
        self.np_, self.ns = bp * lp, bs * ls
        self.nt = self.np_ + self.ns
        self.tm = _pow2_tile(math.gcd(self.np_, ls), 512)
        assert lp & (lp - 1) == 0 and ls & (ls - 1) == 0, "sequence lengths must be powers of two"
        assert self.np_ % ls == 0, "sample rows must start on a sequence-length boundary"
        assert bs + 1 <= SUBLANES

    def row_tile(self, pref):
        return _pow2_tile(math.gcd(self.np_, self.ls), pref)

    def modrow(self, i, tm):
        r0 = i * tm
        return jnp.where(r0 < self.np_, 0, 1 + (r0 - self.np_) // self.ls)

    def seq_len(self, i, tm):
        return jnp.where(i * tm < self.np_, self.lp, self.ls)

    def mirror_tile(self, i, tm):
        def mirrored(tiles_per_seq):
            return (i // tiles_per_seq) * tiles_per_seq + (tiles_per_seq - 1 - i % tiles_per_seq)
        return jnp.where(i * tm < self.np_, mirrored(self.lp // tm), mirrored(self.ls // tm))

    @staticmethod
    def streams_per_group(dr):
        return LANES // (dr // RW_HEAD)

    def stream_slot(self, i, tm, direction, dr):
        r0 = i * tm
        batch = jnp.where(r0 < self.np_, r0 // self.lp, (r0 - self.np_) // self.ls)
        nbatch = jnp.where(r0 < self.np_, self.bp, self.bs)
        return (direction * nbatch + batch) % self.streams_per_group(dr)


def _adaln_kernel(c_ref, w_ref, b_ref, o_ref):
    c = c_ref[...]
    s = c * jax.nn.sigmoid(c)
    o_ref[0] = jnp.dot(s, w_ref[0], precision=HI, preferred_element_type=F32) + b_ref[0]


def _adaln(cond, w_ada, b_ada):
    depth, d, n = w_ada.shape
    tn = _col_tile(n, 1024)
    return pl.pallas_call(
        _adaln_kernel,
        grid=(depth, n // tn),
        in_specs=[pl.BlockSpec((SUBLANES, d), lambda l, j: (0, 0)),
                  pl.BlockSpec((1, d, tn), lambda l, j: (l, 0, j)),
                  pl.BlockSpec((1, 1, tn), lambda l, j: (l, 0, j))],
        out_specs=pl.BlockSpec((1, SUBLANES, tn), lambda l, j: (l, 0, j)),
        out_shape=jax.ShapeDtypeStruct((depth, SUBLANES, n), F32),
        compiler_params=_cparams("parallel", "arbitrary"),
        name="adaln",
    )(cond, w_ada, b_ada.reshape(depth, 1, n))


def _norm_mm_kernel(x_ref, g_ref, sh_ref, sc_ref, w_ref, o_ref, xn_ref):
    @pl.when(pl.program_id(1) == 0)
    def _():
        x = x_ref[...]
        ms = jnp.mean(x * x, axis=-1, keepdims=True)
        y = x * lax.rsqrt(ms + EPS) * g_ref[...]
        xn_ref[...] = (y * (1.0 + sc_ref[0]) + sh_ref[0]).astype(BF16)

    o_ref[...] = jnp.dot(xn_ref[...], w_ref[...], preferred_element_type=F32).astype(o_ref.dtype)


def _norm_mm(geo, x, g, mods, which_shift, w, out_dtype=F32, tm_pref=1024, tn_pref=512):
    nt, d = x.shape
    n = w.shape[1]
    tm = geo.row_tile(tm_pref)
    tn = _col_tile(n, tn_pref)

    def mod_map(which):
        return lambda i, j: (geo.modrow(i, tm) * 6 + which, 0, 0)

    return pl.pallas_call(
        _norm_mm_kernel,
        grid=(nt // tm, n // tn),
        in_specs=[pl.BlockSpec((tm, d), lambda i, j: (i, 0)),
                  pl.BlockSpec((1, d), lambda i, j: (0, 0)),
                  pl.BlockSpec((1, 1, d), mod_map(which_shift)),
                  pl.BlockSpec((1, 1, d), mod_map(which_shift + 1)),
                  pl.BlockSpec((d, tn), lambda i, j: (0, j))],
        out_specs=pl.BlockSpec((tm, tn), lambda i, j: (i, j)),
        out_shape=jax.ShapeDtypeStruct((nt, n), out_dtype),
        scratch_shapes=[pltpu.VMEM((tm, d), BF16)],
        compiler_params=_cparams("parallel", "arbitrary"),
        name="norm_mm",
    )(x, g.reshape(1, d), mods, mods, w)


def _res_mm_kernel(a_ref, w_ref, res_ref, gt_ref, o_ref):
    acc = jnp.dot(a_ref[...], w_ref[...], preferred_element_type=F32)
    o_ref[...] = res_ref[...] + gt_ref[0] * acc


def _res_mm(geo, a, w, res, mods, which_gate, tm_pref=1024, tn_pref=512):
    nt, k = a.shape
    n = w.shape[1]
    tm = geo.row_tile(tm_pref)
    tn = _col_tile(n, tn_pref)
    return pl.pallas_call(
        _res_mm_kernel,
        grid=(nt // tm, n // tn),
        in_specs=[pl.BlockSpec((tm, k), lambda i, j: (i, 0)),
                  pl.BlockSpec((k, tn), lambda i, j: (0, j)),
                  pl.BlockSpec((tm, tn), lambda i, j: (i, j)),
                  pl.BlockSpec((1, 1, tn), lambda i, j: (geo.modrow(i, tm) * 6 + which_gate, 0, j))],
        out_specs=pl.BlockSpec((tm, tn), lambda i, j: (i, j)),
        out_shape=jax.ShapeDtypeStruct((nt, n), F32),
        compiler_params=_cparams("parallel", "arbitrary"),
        name="res_mm",
    )(a, w, res, mods)


def _final_norm_kernel(x_ref, g_ref, o_ref):
    x = x_ref[...]
    ms = jnp.mean(x * x, axis=-1, keepdims=True)
    o_ref[...] = x * lax.rsqrt(ms + EPS) * g_ref[...]


def _final_norm(geo, x, g):
    nt, d = x.shape
    tm = geo.tm
    return pl.pallas_call(
        _final_norm_kernel,
        grid=(nt // tm,),
        in_specs=[pl.BlockSpec((tm, d), lambda i: (i, 0)), pl.BlockSpec((1, d), lambda i: (0, 0))],
        out_specs=pl.BlockSpec((tm, d), lambda i: (i, 0)),
        out_shape=jax.ShapeDtypeStruct((nt, d), F32),
        compiler_params=_cparams("parallel"),
        name="final_norm",
    )(x, g.reshape(1, d))


def _seq_pos(geo, i, tm):
    lseq = geo.seq_len(i, tm)
    rid = lax.broadcasted_iota(jnp.int32, (tm, 1), 0) + i * tm
    return rid & (lseq - 1), lseq


def _neighbours(x, prev_row, next_row, tpos, lseq):
    tm = x.shape[0]
    rid = lax.broadcasted_iota(jnp.int32, (tm, 1), 0)
    xm = pltpu.roll(x, 1, axis=0)
    xm = jnp.where(rid == 0, prev_row, xm)
    xm = jnp.where(tpos == 0, 0.0, xm)
    xp = pltpu.roll(x, tm - 1, axis=0)
    xp = jnp.where(rid == tm - 1, next_row, xp)
    xp = jnp.where(tpos == lseq - 1, 0.0, xp)
    return xm, xp


def _halo_specs(nt, tm, width, col):
    r = tm // SUBLANES
    last = nt // SUBLANES - 1
    prev = pl.BlockSpec((SUBLANES, width), lambda i, j: (jnp.maximum(i * r - 1, 0), col(j)))
    nxt = pl.BlockSpec((SUBLANES, width), lambda i, j: (jnp.minimum((i + 1) * r, last), col(j)))
    return prev, nxt


def _hy_prep_kernel(geo, tm, *refs):
    (x0_ref, x0p_ref, x0n_ref, x1_ref, x1p_ref, x1n_ref, x2_ref, x2p_ref, x2n_ref,
     w_ref, b_ref, o0_ref, u_ref) = refs
    i = pl.program_id(0)
    tpos, lseq = _seq_pos(geo, i, tm)

    def conv(x_ref, p_ref, n_ref, g):
        x = x_ref[...]
        xm, xp = _neighbours(x, p_ref[SUBLANES - 1:SUBLANES, :], n_ref[0:1, :], tpos, lseq)
        return xm * w_ref[0, g] + x * w_ref[1, g] + xp * w_ref[2, g] + b_ref[g]

    c0 = conv(x0_ref, x0p_ref, x0n_ref, 0)
    c1 = conv(x1_ref, x1p_ref, x1n_ref, 1)
    c2 = conv(x2_ref, x2p_ref, x2n_ref, 2)
    o0_ref[...] = c0
    u_ref[...] = c1 * c2


def _hy_prep(geo, p_hy, short_w, short_b):
    nt = p_hy.shape[0]
    dh = p_hy.shape[1] // 3
    tm = _pow2_tile(geo.tm, 256)
    tc = _col_tile(dh, 512)
    ncb = dh // tc
    specs = []
    for g in range(3):
        col = (lambda j, g=g: g * ncb + j)
        specs.append(pl.BlockSpec((tm, tc), lambda i, j, col=col: (i, col(j))))
        specs.extend(_halo_specs(nt, tm, tc, col))
    specs.append(pl.BlockSpec((3, 3, 1, tc), lambda i, j: (0, 0, 0, j)))
    specs.append(pl.BlockSpec((3, 1, tc), lambda i, j: (0, 0, j)))
    out_spec = pl.BlockSpec((tm, tc), lambda i, j: (i, j))
    w4 = short_w.reshape(3, 3, 1, dh)
    b3 = short_b.reshape(3, 1, dh)
    return pl.pallas_call(
        functools.partial(_hy_prep_kernel, geo, tm),
        grid=(nt // tm, ncb),
        in_specs=specs,
        out_specs=[out_spec, out_spec],
        out_shape=[jax.ShapeDtypeStruct((nt, dh), F32)] * 2,
        compiler_params=_cparams("parallel", "arbitrary"),
        name="hy_prep",
    )(*([p_hy] * 9), w4, b3)


def _lmm_kernel(has_epi, *refs):
    if has_epi:
        a_ref, x_ref, x0_ref, u_ref, bias_ref, o_ref = refs
    else:
        a_ref, x_ref, o_ref = refs
    acc = jnp.dot(a_ref[0], x_ref[0], precision=HI, preferred_element_type=F32)
    if has_epi:
        acc = x0_ref[0] * (acc + u_ref[0] * bias_ref[...])
    o_ref[0] = acc.astype(o_ref.dtype)


def _lmm(a, x, epi=None, out_dtype=F32):
    gm, m, k = a.shape
    g, _, n = x.shape
    tn = _col_tile(n, 2048)
    xspec = pl.BlockSpec((1, k, tn), lambda gi, j: (gi, 0, j))
    ospec = pl.BlockSpec((1, m, tn), lambda gi, j: (gi, 0, j))
    specs = [pl.BlockSpec((1, m, k), lambda gi, j: (gi % gm, 0, 0)), xspec]
    args = [a, x]
    if epi is not None:
        x0, u, bias = epi
        specs += [ospec, ospec, pl.BlockSpec((1, tn), lambda gi, j: (0, j))]
        args += [x0, u, bias]
    return pl.pallas_call(
        functools.partial(_lmm_kernel, epi is not None),
        grid=(g, n // tn),
        in_specs=specs,
        out_specs=ospec,
        out_shape=jax.ShapeDtypeStruct((g, m, n), out_dtype),
        compiler_params=_cparams("parallel", "arbitrary"),
        name="dft_lmm",
    )(*args)


def _spec_kernel(has_epi, *refs):
    if has_epi:
        mf_ref, h_ref, mi_ref, x_ref, x0_ref, bias_ref, o_ref = refs
    else:
        mf_ref, h_ref, mi_ref, x_ref, o_ref = refs
    x = x_ref[0]
    f = jnp.dot(mf_ref[0], x, precision=HI, preferred_element_type=F32)
    r = f.shape[0] // 2
    fr, fi = f[:r], f[r:]
    hr, hi = h_ref[0, :r], h_ref[0, r:]
    y = jnp.concatenate([fr * hr - fi * hi, fr * hi + fi * hr], axis=0)
    out = jnp.dot(mi_ref[0], y, precision=HI, preferred_element_type=F32)
    if has_epi:
        out = x0_ref[0] * (out + x * bias_ref[...])
    o_ref[0] = out.astype(o_ref.dtype)


def _spectral(mf, h, mi, x, epi=None, out_dtype=F32):
    gm, r2, k = mf.shape
    kout = mi.shape[1]
    g, _, c = x.shape
    tc = _col_tile(c, 512)
    xspec = pl.BlockSpec((1, k, tc), lambda gi, j: (gi, 0, j))
    ospec = pl.BlockSpec((1, kout, tc), lambda gi, j: (gi, 0, j))
    specs = [pl.BlockSpec((1, r2, k), lambda gi, j: (gi % gm, 0, 0)),
             pl.BlockSpec((1, r2, tc), lambda gi, j: (gi % gm, 0, j)),
             pl.BlockSpec((1, kout, r2), lambda gi, j: (gi % gm, 0, 0)),
             xspec]
    args = [mf, h, mi, x]
    if epi is not None:
        x0, bias = epi
        specs += [ospec, pl.BlockSpec((1, tc), lambda gi, j: (0, j))]
        args += [x0, bias]
    return pl.pallas_call(
        functools.partial(_spec_kernel, epi is not None),
        grid=(g, c // tc),
        in_specs=specs,
        out_specs=ospec,
        out_shape=jax.ShapeDtypeStruct((g, kout, c), out_dtype),
        compiler_params=_cparams("parallel", "arbitrary"),
        name="dft_spectral",
    )(*args)


def _dft_tables_direct(l):
    n = 2 * l
    nf = l + 1
    r = _round_up(nf, SUBLANES)
    kk = np.arange(nf)[:, None].astype(np.float64)
    ang_half = 2.0 * np.pi * ((kk * np.arange(l)[None, :]) % n) / n
    ang_full = 2.0 * np.pi * ((kk * np.arange(n)[None, :]) % n) / n

    def fwd(ang):
        m = np.zeros((2 * r, ang.shape[1]))
        m[:nf] = np.cos(ang)
        m[r:r + nf] = -np.sin(ang)
        return m

    coef = np.full((nf,), 2.0 / n)
    coef[0] = coef[-1] = 1.0 / n
    inv = np.zeros((l, 2 * r))
    inv[:, :nf] = (np.cos(ang_half) * coef[:, None]).T
    inv[:, r:r + nf] = (-np.sin(ang_half) * coef[:, None]).T
    f32 = lambda a: jnp.asarray(a[None], F32)
    return f32(fwd(ang_half)), f32(fwd(ang_full)), f32(inv)


def _dft_tables_two_level(l):
    n = 2 * l
    n1 = FFT_MINOR
    n2 = n // n1
    nb = n2 // 2 + 1
    nbp = _round_up(nb, 4)
    k2 = np.arange(nb)[:, None].astype(np.float64)
    ang2 = 2.0 * np.pi * ((k2 * np.arange(n2)[None, :]) % n2) / n2
    ms_full = np.zeros((2 * nbp, n2))
    ms_full[0:2 * nb:2] = np.cos(ang2)
    ms_full[1:2 * nb:2] = -np.sin(ang2)
    ms_half = ms_full[:, :n2 // 2]
    coef = np.full((nb,), 2.0 / n)
    coef[0] = coef[-1] = 1.0 / n
    ms_inv = np.zeros((n2 // 2, 2 * nbp))
    ms_inv[:, 0:2 * nb:2] = (np.cos(ang2[:, :n2 // 2]) * coef[:, None]).T
    ms_inv[:, 1:2 * nb:2] = (-np.sin(ang2[:, :n2 // 2]) * coef[:, None]).T

    a_fwd = np.zeros((nbp, 2 * n1, 2 * n1))
    a_inv = np.zeros((nbp, 2 * n1, 2 * n1))
    i1 = np.arange(n1).astype(np.float64)
    for b in range(nb):
        ph = 2.0 * np.pi * (((np.outer(i1, i1) * n2) + (i1[None, :] * b)) % n) / n
        wr, wi = np.cos(ph), -np.sin(ph)
        a_fwd[b] = np.block([[wr, -wi], [wi, wr]])
        ph = 2.0 * np.pi * (((np.outer(i1, i1) * n2) + (i1[:, None] * b)) % n) / n
        pr, pi_ = np.cos(ph), np.sin(ph)
        a_inv[b] = np.block([[pr, -pi_], [pi_, pr]])
    f32 = lambda a: jnp.asarray(a, F32)
    return (f32(ms_half[None]), f32(ms_full[None]), f32(ms_inv[None]), f32(a_fwd), f32(a_inv), n1, n2, nbp)


def _hyena_filter(l, w1, b1, w2, b2, w3, freq, dh):
    pos = jnp.arange(l, dtype=F32)[:, None]
    t = pos / max(l - 1, 1)
    emb = w1.shape[0]
    bands = (emb - 1) // 2
    fb = jnp.linspace(1e-4, bands - 1, bands, dtype=F32)
    ang = (2.0 * math.pi / l) * pos * fb
    z = jnp.concatenate([t, jnp.cos(ang), -jnp.sin(ang)], axis=-1)
    h = jnp.sin(freq[0] * (jnp.dot(z, w1, precision=HI) + b1))
    h = jnp.sin(freq[1] * (jnp.dot(h, w2, precision=HI) + b2))
    h = jnp.dot(h, w3, precision=HI).reshape(l, 2, dh)
    deltas = jnp.abs(jnp.linspace(math.log(HY_TARGET) / HY_SLOW, math.log(HY_TARGET) / HY_FAST, dh, dtype=F32))
    h = h * jnp.exp(-t[:, :, None] * deltas)
    filt = jnp.concatenate([h[:, 0], jnp.zeros((1, dh), F32), jnp.flip(h[1:, 1], axis=0)], axis=0)
    return filt / jnp.sum(jnp.abs(filt), axis=0, keepdims=True)


def _hyena_long_conv(geo, x0, u, filt_p, filt_s, bias):
    dh = u.shape[1]
    bias2 = bias.reshape(1, dh)
    mf_half, mf_full, mi = _dft_tables_direct(geo.lp)
    h_p = _lmm(mf_full, filt_p[None])
    up = u[:geo.np_].reshape(geo.bp, geo.lp, dh)
    x0p = x0[:geo.np_].reshape(geo.bp, geo.lp, dh)
    y_p = _spectral(mf_half, h_p, mi, up, epi=(x0p, bias2), out_dtype=BF16)
    ms_half, ms_full, ms_inv, a_fwd, a_inv, n1, n2, nbp = _dft_tables_two_level(geo.ls)
    hs = _lmm(ms_full, filt_s.reshape(1, n2, n1 * dh))
    hs = _lmm(a_fwd, hs.reshape(nbp, 2 * n1, dh))
    us = u[geo.np_:].reshape(geo.bs, n2 // 2, n1 * dh)
    x0s = x0[geo.np_:].reshape(geo.bs, n2 // 2, n1 * dh)
    xs = _lmm(ms_half, us)
    zs = _spectral(a_fwd, hs, a_inv, xs.reshape(geo.bs * nbp, 2 * n1, dh))
    zs = zs.reshape(geo.bs, 2 * nbp, n1 * dh)
    y_s = _lmm(ms_inv, zs, epi=(x0s, us, jnp.tile(bias2, (1, n1))), out_dtype=BF16)
    return jnp.concatenate([y_p.reshape(geo.np_, dh), y_s.reshape(geo.ns, dh)], axis=0)


def _head_sum(x):
    dr = x.shape[1]
    nheads = dr // RW_HEAD
    assert LANES % nheads == 0
    t = x[:, 0:LANES]
    for q in range(1, dr // LANES):
        t = t + x[:, q * LANES:(q + 1) * LANES]
    shift = LANES // 2
    while shift >= nheads:
        t = t + pltpu.roll(t, shift, axis=1)
        shift //= 2
    return jnp.concatenate([t] * (dr // LANES), axis=1)


def _flip_rows(x):
    tm, c = x.shape
    x3 = x.reshape(tm // SUBLANES, SUBLANES, c)
    rid = lax.broadcasted_iota(jnp.int32, (1, SUBLANES, 1), 1)
    for s in (4, 2, 1):
        x3 = jnp.where((rid & s) != 0, pltpu.roll(x3, s, axis=1), pltpu.roll(x3, SUBLANES - s, axis=1))
    return jnp.concatenate([x3[j] for j in reversed(range(tm // SUBLANES))], axis=0)


def _rw_prep_kernel(geo, tm, dr, *refs):
    (x_ref, xp_ref, xn_ref, mu_ref, w0_ref, w2_ref, a0_ref, a2_ref, g2_ref, kk_ref, ka_ref, rk_ref,
     rf_o, nkkf_o, vf_o, wf_o, kxf_o, bf_o, rb_o, nkkb_o, vb_o, wb_o, kxb_o, bb_o, g_o, bon_o) = refs
    i = pl.program_id(0)
    tpos, lseq = _seq_pos(geo, i, tm)
    x = x_ref[...]
    xm, xp = _neighbours(x, xp_ref[SUBLANES - 1:SUBLANES, :], xn_ref[0:1, :], tpos, lseq)
    x = x + mu_ref[...] * (0.5 * (xm + xp) - x)
    r = x[:, 0:dr]
    k = x[:, dr:2 * dr]
    v = x[:, 2 * dr:3 * dr]
    wd = x[:, 3 * dr:3 * dr + LANES]
    ad = x[:, 3 * dr + LANES:3 * dr + 2 * LANES]
    gd = x[:, 3 * dr + 2 * LANES:3 * dr + 3 * LANES]
    wl = w0_ref[...] + jnp.dot(jnp.tanh(wd), w2_ref[...], precision=HI, preferred_element_type=F32)
    w_log = -jax.nn.softplus(-wl) - 0.5
    decay = jnp.exp(-jnp.exp(w_log))
    a = jax.nn.sigmoid(a0_ref[...] + jnp.dot(ad, a2_ref[...], precision=HI, preferred_element_type=F32))
    g_o[...] = jnp.dot(jax.nn.sigmoid(gd), g2_ref[...], precision=HI, preferred_element_type=F32)
    kk = k * kk_ref[...]
    kk = kk * lax.rsqrt(_head_sum(kk * kk) + 1e-12)
    a_0, a_1 = a[:, :dr], a[:, dr:]
    kx0 = k * (1.0 + (a_0 - 1.0) * ka_ref[...])
    kx1 = k * (1.0 + (a_1 - 1.0) * ka_ref[...])
    bon_o[...] = _head_sum(r * (kx0 + kx1) * rk_ref[...]) * v

    slab = LANES // geo.streams_per_group(dr)

    def emit(o_ref, val, direction):
        if direction:
            val = _flip_rows(val)
        shift = geo.stream_slot(i, tm, direction, dr) * slab
        for q in range(dr // LANES):
            o_ref[:, q * LANES:(q + 1) * LANES] = pltpu.roll(val[:, q * LANES:(q + 1) * LANES], shift, axis=1)

    nkk = -kk
    for o_f, o_b, val_f, val_b in ((rf_o, rb_o, r, r), (nkkf_o, nkkb_o, nkk, nkk), (vf_o, vb_o, v, v),
                                   (wf_o, wb_o, decay[:, :dr], decay[:, dr:]), (kxf_o, kxb_o, kx0, kx1),
                                   (bf_o, bb_o, kk * a_0, kk * a_1)):
        emit(o_f, val_f, 0)
        emit(o_b, val_b, 1)


def _blockdiag2(w):
    _, r, c = w.shape
    z = jnp.zeros((r, c), w.dtype)
    return jnp.concatenate([jnp.concatenate([w[0], z], axis=1), jnp.concatenate([z, w[1]], axis=1)], axis=0)


def _head_minor_perm(dr):
    return np.arange(dr).reshape(dr // RW_HEAD, RW_HEAD).T.reshape(-1)


def _rw_prep(geo, p_rw, mu, w0, w2, a0, a2, g2, k_k, k_a, r_k):
    nt, width = p_rw.shape
    dr = k_k.shape[0]
    assert 2 * w2.shape[1] == LANES and 2 * a2.shape[1] == LANES and g2.shape[0] == LANES
    tm = _pow2_tile(min(geo.tm, geo.lp), 128)
    full = lambda shape: pl.BlockSpec(shape, lambda i, j: tuple(0 for _ in shape))
    prev, nxt = _halo_specs(nt, tm, width, lambda j: 0)
    consts = [mu.reshape(1, width), w0.reshape(1, 2 * dr), _blockdiag2(w2), a0.reshape(1, 2 * dr),
              _blockdiag2(a2), g2, k_k.reshape(1, dr), k_a.reshape(1, dr), r_k.reshape(1, dr)]
    ospec = pl.BlockSpec((tm, dr), lambda i, j: (i, 0))
    mspec = pl.BlockSpec((tm, dr), lambda i, j: (geo.mirror_tile(i, tm), 0))
    return pl.pallas_call(
        functools.partial(_rw_prep_kernel, geo, tm, dr),
        grid=(nt // tm, 1),
        in_specs=[pl.BlockSpec((tm, width), lambda i, j: (i, 0)), prev, nxt] + [full(c.shape) for c in consts],
        out_specs=[ospec] * 6 + [mspec] * 6 + [ospec] * 2,
        out_shape=[jax.ShapeDtypeStruct((nt, dr), F32)] * 14,
        compiler_params=_cparams("parallel", "arbitrary"),
        name="rw_prep",
    )(p_rw, p_rw, p_rw, *consts)


def _rw_scan_kernel(tc, r_ref, nkk_ref, v_ref, w_ref, kx_ref, b_ref, s0_ref, y_ref, so_ref, s_ref):
    t_chunk = pl.program_id(1)

    @pl.when(t_chunk == 0)
    def _():
        s_ref[...] = s0_ref[...]

    nacc = 4

    def step(t, carry):
        acc = [jnp.zeros((RW_HEAD, LANES), F32) for _ in range(nacc)]
        for k in range(RW_HEAD):
            acc[k % nacc] = acc[k % nacc] + s_ref[k] * nkk_ref[t, k:k + 1, :]
        sa = (acc[0] + acc[1]) + (acc[2] + acc[3])
        vv = v_ref[t]
        acc = [jnp.zeros((RW_HEAD, LANES), F32) for _ in range(nacc)]
        for k in range(RW_HEAD):
            new = s_ref[k] * w_ref[t, k:k + 1, :] + sa * b_ref[t, k:k + 1, :] + vv * kx_ref[t, k:k + 1, :]
            s_ref[k] = new
            acc[k % nacc] = acc[k % nacc] + new * r_ref[t, k:k + 1, :]
        y_ref[t] = (acc[0] + acc[1]) + (acc[2] + acc[3])
        return carry

    lax.fori_loop(0, tc, step, 0)

    @pl.when(t_chunk == pl.num_programs(1) - 1)
    def _():
        so_ref[...] = s_ref[...]


def _rw_scan(r, nkk, v, w, kx, b, s0):
    t_len, n, lanes = r.shape
    tc = _pow2_tile(t_len, 32)
    xspec = pl.BlockSpec((tc, n, LANES), lambda g, t: (t, 0, g))
    sspec = pl.BlockSpec((n, n, LANES), lambda g, t: (0, 0, g))
    return pl.pallas_call(
        functools.partial(_rw_scan_kernel, tc),
        grid=(lanes // LANES, t_len // tc),
        in_specs=[xspec] * 6 + [sspec],
        out_specs=[xspec, sspec],
        out_shape=[jax.ShapeDtypeStruct((t_len, n, lanes), F32), jax.ShapeDtypeStruct((n, n, lanes), F32)],
        scratch_shapes=[pltpu.VMEM((n, n, LANES), F32)],
        compiler_params=_cparams("parallel", "arbitrary"),
        name="rw_scan",
    )(r, nkk, v, w, kx, b, s0)


def _merge_slabs(pieces, slab, offset):
    n = len(pieces)
    lane_slab = lax.broadcasted_iota(jnp.int32, pieces[0].shape, 1) // slab
    acc = pieces[n - 1]
    for p in range(n - 1):
        acc = jnp.where(lane_slab == (offset + p) % n, pieces[p], acc)
    return acc


def _to_streams_kernel(nsrc, ntens, slab, *refs):
    srcs, outs = refs[:ntens * nsrc], refs[ntens * nsrc:]
    for ti in range(ntens):
        tiles = [srcs[ti * nsrc + s][...] for s in range(nsrc)]
        for j in range(SUBLANES):
            merged = _merge_slabs(tiles, slab, j)
            shift = ((nsrc - j) % nsrc) * slab
            outs[ti][:, j, :] = pltpu.roll(merged, shift, axis=1) if shift else merged


def _to_streams(fwd, bwd, row0, nbatch, l, direction):
    ntens = len(fwd)
    dr = fwd[0].shape[1]
    nsrc = _Geo.streams_per_group(dr)
    assert nsrc == SUBLANES, "the stream layout assumes 16 heads (eight channels per 128-lane tile)"
    slab = LANES // nsrc
    tt = _pow2_tile(l, 128)
    if direction is None:
        assert 2 * nbatch == nsrc
        ngroups = 1
        source = lambda g, s: (s // nbatch, s % nbatch)
    else:
        assert nbatch % nsrc == 0
        ngroups = nbatch // nsrc
        source = lambda g, s: (direction, g * nsrc + s)
    specs, args = [], []
    for ti in range(ntens):
        for s in range(nsrc):
            d = source(0, s)[0]
            specs.append(pl.BlockSpec(
                (tt, LANES), lambda g, i, q, s=s: ((row0 + source(g, s)[1] * l) // tt + i, q)))
            args.append((bwd if d else fwd)[ti])
    ospec = pl.BlockSpec((tt, SUBLANES, LANES), lambda g, i, q: (i, q, g))
    return pl.pallas_call(
        functools.partial(_to_streams_kernel, nsrc, ntens, slab),
        grid=(ngroups, l // tt, dr // LANES),
        in_specs=specs,
        out_specs=[ospec] * ntens,
        out_shape=[jax.ShapeDtypeStruct((l, RW_HEAD, ngroups * LANES), F32)] * ntens,
        compiler_params=_cparams("parallel", "parallel", "arbitrary"),
        name="to_streams",
    )(*args)


def _from_streams_kernel(nsrc, slab, y_ref, o_ref):
    rolled = [y_ref[:, j, :] if j == 0 else pltpu.roll(y_ref[:, j, :], j * slab, axis=1) for j in range(SUBLANES)]
    for s in range(nsrc):
        merged = _merge_slabs(rolled, slab, s)
        shift = ((nsrc - s) % nsrc) * slab
        o_ref[s] = pltpu.roll(merged, shift, axis=1) if shift else merged


def _from_streams(y, dr):
    l, _, lanes = y.shape
    nsrc = _Geo.streams_per_group(dr)
    slab = LANES // nsrc
    tt = _pow2_tile(l, 128)
    ngroups = lanes // LANES
    return pl.pallas_call(
        functools.partial(_from_streams_kernel, nsrc, slab),
        grid=(ngroups, l // tt, dr // LANES),
        in_specs=[pl.BlockSpec((tt, SUBLANES, LANES), lambda g, i, q: (i, q, g))],
        out_specs=pl.BlockSpec((nsrc, tt, LANES), lambda g, i, q: (g, i, q)),
        out_shape=jax.ShapeDtypeStruct((ngroups * nsrc, l, dr), F32),
        compiler_params=_cparams("parallel", "parallel", "arbitrary"),
        name="from_streams",
    )(y)


def _rw_post_kernel(yf_ref, yb_ref, bon_ref, g_ref, lnw_ref, lnb_ref, o_ref):
    y = yf_ref[...] + _flip_rows(yb_ref[...])
    inv_n = 1.0 / RW_HEAD
    mean = _head_sum(y) * inv_n
    yc = y - mean
    var = _head_sum(yc * yc) * inv_n
    yn = yc * lax.rsqrt(var + RW_GN_EPS) * lnw_ref[...] + lnb_ref[...]
    o_ref[...] = ((yn + bon_ref[...]) * g_ref[...]).astype(o_ref.dtype)


def _rw_post(geo, yf, yb, bonus, g, ln_w, ln_b):
    nt, dr = yf.shape
    tm = _pow2_tile(min(geo.tm, geo.lp), 256)
    tspec = pl.BlockSpec((tm, dr), lambda i: (i, 0))
    mspec = pl.BlockSpec((tm, dr), lambda i: (geo.mirror_tile(i, tm), 0))
    full = lambda a: pl.BlockSpec(a.shape, lambda i: (0, 0))
    consts = [ln_w.reshape(1, dr), ln_b.reshape(1, dr)]
    return pl.pallas_call(
        _rw_post_kernel,
        grid=(nt // tm,),
        in_specs=[tspec, mspec, tspec, tspec] + [full(c) for c in consts],
        out_specs=tspec,
        out_shape=jax.ShapeDtypeStruct((nt, dr), BF16),
        compiler_params=_cparams("parallel"),
        name="rw_post",
    )(yf, yb, bonus, g, *consts)


def _rwkv_mixer(geo, p_rw, s0_sample, mu, w0, w2, a0, a2, g2, k_k, k_a, r_k, ln_w, ln_b):
    dr = k_k.shape[0]
    h = dr // RW_HEAD
    outs = _rw_prep(geo, p_rw, mu, w0, w2, a0, a2, g2, k_k, k_a, r_k)
    fwd, bwd, (g, bonus) = outs[0:6], outs[6:12], outs[12:14]
    s0 = s0_sample.transpose(4, 3, 1, 0, 2).reshape(RW_HEAD, RW_HEAD, 2 * geo.bs * h)
    y_s, _ = _rw_scan(*_to_streams(fwd, bwd, geo.np_, geo.bs, geo.ls, None), s0)
    y_s = _from_streams(y_s, dr)
    y_p, s_p = [], []
    for direction in range(2):
        zero = jnp.zeros((RW_HEAD, RW_HEAD, geo.bp * h), F32)
        y_d, s_d = _rw_scan(*_to_streams(fwd, bwd, 0, geo.bp, geo.lp, direction), zero)
        y_p.append(_from_streams(y_d, dr))
        s_p.append(s_d.reshape(RW_HEAD, RW_HEAD, geo.bp, h))
    states = jnp.stack(s_p).transpose(3, 0, 4, 2, 1)
    yf = jnp.concatenate([y_p[0].reshape(geo.np_, dr), y_s[:geo.bs].reshape(geo.ns, dr)], axis=0)
    yb = jnp.concatenate([y_p[1].reshape(geo.np_, dr), y_s[geo.bs:].reshape(geo.ns, dr)], axis=0)
    return _rw_post(geo, yf, yb, bonus, g, ln_w, ln_b), states


def _gla_dir(q, k, v, gkd, gk2, gkb, s_ref, o_ref, reverse):
    c, dk = q.shape
    hk, hv = dk // GLA_H, v.shape[1] // GLA_H
    z = jnp.dot(gkd, gk2, precision=HI, preferred_element_type=F32) + gkb
    lg = (jnp.minimum(z, 0.0) - jnp.log1p(jnp.exp(-jnp.abs(z)))) * (1.0 / GLA_GATE_NORM)
    row = lax.broadcasted_iota(jnp.int32, (c, c), 0)
    col = lax.broadcasted_iota(jnp.int32, (c, c), 1)
    tri = (col >= row) if reverse else (col <= row)
    b = jnp.dot(tri.astype(F32), lg, precision=HI, preferred_element_type=F32)
    i_ref = c // 2 if reverse else c // 2 - 1
    i_last = 0 if reverse else c - 1
    b_ref = b[i_ref:i_ref + 1, :]
    q_in = (q * jnp.exp(b - b_ref)).astype(BF16)
    k_in = (k * jnp.exp(b_ref - b)).astype(BF16)
    q_dec = (q * jnp.exp(b)).astype(BF16)
    bt = b.T
    bt_last = bt[:, i_last:i_last + 1]
    k_dec_t = (k.T * jnp.exp(bt_last - bt)).astype(BF16)
    chunk_decay = jnp.exp(bt_last)
    vb = v.astype(BF16)
    for h in range(GLA_H):
        ks = slice(h * hk, (h + 1) * hk)
        vs = slice(h * hv, (h + 1) * hv)
        s = s_ref[h]
        scores = lax.dot_general(q_in[:, ks], k_in[:, ks], (((1,), (1,)), ((), ())), preferred_element_type=F32)
        scores = jnp.where(tri, scores, 0.0).astype(BF16)
        o_ref[:, vs] = (jnp.dot(scores, vb[:, vs], preferred_element_type=F32)
                        + jnp.dot(q_dec[:, ks], s.astype(BF16), preferred_element_type=F32))
        s_ref[h] = s * chunk_decay[ks] + jnp.dot(k_dec_t[ks], vb[:, vs], preferred_element_type=F32)


def _gla_kernel(scale, *refs):
    (qf_ref, kf_ref, vf_ref, df_ref, qb_ref, kb_ref, vb_ref, db_ref, gk2_ref, gkb_ref, s0_ref,
     of_ref, ob_ref, so_ref, sf, sb) = refs
    ci = pl.program_id(1)

    @pl.when(ci == 0)
    def _():
        sf[...] = s0_ref[0, 0]
        sb[...] = s0_ref[0, 1]

    rank = gk2_ref.shape[1]
    _gla_dir(qf_ref[...] * scale, kf_ref[...], vf_ref[...], df_ref[:, 0:rank], gk2_ref[0], gkb_ref[0],
             sf, of_ref, False)
    _gla_dir(qb_ref[...] * scale, kb_ref[...], vb_ref[...], db_ref[:, rank:2 * rank], gk2_ref[1], gkb_ref[1],
             sb, ob_ref, True)

    @pl.when(ci == pl.num_programs(1) - 1)
    def _():
        so_ref[0, 0] = sf[...]
        so_ref[0, 1] = sb[...]


def _gla_scan(p, gkd, row0, b, l, dk, dv, gk2, gk_b, s0):
    hk, hv = dk // GLA_H, dv // GLA_H
    c = GLA_CHUNK
    n = l // c
    blk0 = row0 // c
    rank = gk2.shape[1]
    assert (2 * dk) % dv == 0

    def tok(bi, ci, rev):
        return blk0 + bi * n + (n - 1 - ci if rev else ci)

    def specs(rev):
        return [pl.BlockSpec((c, dk), lambda bi, ci: (tok(bi, ci, rev), 0)),
                pl.BlockSpec((c, dk), lambda bi, ci: (tok(bi, ci, rev), 1)),
                pl.BlockSpec((c, dv), lambda bi, ci: (tok(bi, ci, rev), 2 * dk // dv)),
                pl.BlockSpec((c, LANES), lambda bi, ci: (tok(bi, ci, rev), 0))]

    def ospec(rev):
        return pl.BlockSpec((c, dv), lambda bi, ci: (bi * n + (n - 1 - ci if rev else ci), 0))

    sspec = pl.BlockSpec((1, 2, GLA_H, hk, hv), lambda bi, ci: (bi, 0, 0, 0, 0))
    return pl.pallas_call(
        functools.partial(_gla_kernel, float(hk) ** -0.5),
        grid=(b, n),
        in_specs=specs(False) + specs(True) + [
            pl.BlockSpec((2, rank, dk), lambda bi, ci: (0, 0, 0)),
            pl.BlockSpec((2, 1, dk), lambda bi, ci: (0, 0, 0)),
            sspec],
        out_specs=[ospec(False), ospec(True), sspec],
        out_shape=[jax.ShapeDtypeStruct((b * l, dv), F32), jax.ShapeDtypeStruct((b * l, dv), F32),
                   jax.ShapeDtypeStruct((b, 2, GLA_H, hk, hv), F32)],
        scratch_shapes=[pltpu.VMEM((GLA_H, hk, hv), F32), pltpu.VMEM((GLA_H, hk, hv), F32)],
        compiler_params=_cparams("parallel", "arbitrary"),
        name="gla_scan",
    )(p, p, p, gkd, p, p, p, gkd, gk2, gk_b.reshape(2, 1, dk), s0)


def _gla_post_kernel(hv, of_ref, ob_ref, g_ref, nw_ref, o_ref):
    o = of_ref[...] + ob_ref[...]
    g = g_ref[...]
    for h in range(GLA_H):
        oh = o[:, h * hv:(h + 1) * hv]
        gh = g[:, h * hv:(h + 1) * hv]
        oh = oh * lax.rsqrt(jnp.mean(oh * oh, axis=-1, keepdims=True) + EPS) * nw_ref[...]
        o_ref[:, h * hv:(h + 1) * hv] = (oh * (gh * jax.nn.sigmoid(gh))).astype(o_ref.dtype)


def _gla_post(geo, o_f, o_b, p, dk, dv, norm_w):
    nt = o_f.shape[0]
    tm = _pow2_tile(geo.tm, 256)
    hv = dv // GLA_H
    tspec = pl.BlockSpec((tm, dv), lambda i: (i, 0))
    return pl.pallas_call(
        functools.partial(_gla_post_kernel, hv),
        grid=(nt // tm,),
        in_specs=[tspec, tspec, pl.BlockSpec((tm, dv), lambda i: (i, 2 * dk // dv + 1)),
                  pl.BlockSpec((1, hv), lambda i: (0, 0))],
        out_specs=tspec,
        out_shape=jax.ShapeDtypeStruct((nt, dv), BF16),
        compiler_params=_cparams("parallel"),
        name="gla_post",
    )(o_f, o_b, p, norm_w.reshape(1, hv))


def _gla_mixer(geo, p, gkd, s0_sample, gk2, gk_b, norm_w, dk, dv):
    hk, hv = dk // GLA_H, dv // GLA_H
    s0_p = jnp.zeros((geo.bp, 2, GLA_H, hk, hv), F32)
    of_p, ob_p, s_p = _gla_scan(p, gkd, 0, geo.bp, geo.lp, dk, dv, gk2, gk_b, s0_p)
    of_s, ob_s, _ = _gla_scan(p, gkd, geo.np_, geo.bs, geo.ls, dk, dv, gk2, gk_b, s0_sample)
    o_f = jnp.concatenate([of_p, of_s], axis=0)
    o_b = jnp.concatenate([ob_p, ob_s], axis=0)
    return _gla_post(geo, o_f, o_b, p, dk, dv, norm_w), s_p


def _conv_gate_kernel(geo, tm, g_ref, gp_ref, gn_ref, v_ref, w_ref, o_ref):
    i = pl.program_id(0)
    is_prompt = i * tm < geo.np_
    halo = GRID_W
    n_ext = tm + 2 * halo

    def conv(lseq, ncol):
        ext = jnp.concatenate([gp_ref[...], g_ref[...], gn_ref[...]], axis=0)
        epos = (lax.broadcasted_iota(jnp.int32, (n_ext, 1), 0) + (i * tm - halo)) & (lseq - 1)
        scol = epos & (ncol - 1)
        taps = {-1: pltpu.roll(jnp.where(scol == ncol - 1, 0.0, ext), 1, axis=0),
                0: ext,
                1: pltpu.roll(jnp.where(scol == 0, 0.0, ext), n_ext - 1, axis=0)}

        def row_sum(di):
            start = halo + di * GRID_W
            return sum(taps[dj][start:start + tm, :] * w_ref[di + 1, dj + 1] for dj in (-1, 0, 1))

        acc = row_sum(0)
        nrow = lseq // ncol
        if nrow > 1:
            assert ncol == GRID_W
            tpos = (lax.broadcasted_iota(jnp.int32, (tm, 1), 0) + i * tm) & (lseq - 1)
            row = tpos >> int(math.log2(ncol))
            acc = acc + jnp.where(row >= 1, row_sum(-1), 0.0) + jnp.where(row <= nrow - 2, row_sum(1), 0.0)
        o_ref[...] = (acc * jax.nn.sigmoid(acc) * v_ref[...]).astype(o_ref.dtype)

    @pl.when(is_prompt)
    def _():
        conv(geo.lp, geo.lp)

    @pl.when(jnp.logical_not(is_prompt))
    def _():
        conv(geo.ls, GRID_W)


def _conv_gate(geo, u, conv_w):
    nt = u.shape[0]
    f = u.shape[1] // 2
    tm = _pow2_tile(geo.tm, 256)
    assert tm % GRID_W == 0
    tc = _col_tile(f, 512)
    ncb = f // tc
    r = tm // GRID_W
    last = nt // GRID_W - 1
    return pl.pallas_call(
        functools.partial(_conv_gate_kernel, geo, tm),
        grid=(nt // tm, ncb),
        in_specs=[pl.BlockSpec((tm, tc), lambda i, j: (i, j)),
                  pl.BlockSpec((GRID_W, tc), lambda i, j: (jnp.maximum(i * r - 1, 0), j)),
                  pl.BlockSpec((GRID_W, tc), lambda i, j: (jnp.minimum((i + 1) * r, last), j)),
                  pl.BlockSpec((tm, tc), lambda i, j: (i, ncb + j)),
                  pl.BlockSpec((3, 3, 1, tc), lambda i, j: (0, 0, 0, j))],
        out_specs=pl.BlockSpec((tm, tc), lambda i, j: (i, j)),
        out_shape=jax.ShapeDtypeStruct((nt, f), BF16),
        compiler_params=_cparams("parallel", "arbitrary"),
        name="conv_gate",
    )(u, u, u, u, conv_w.reshape(3, 3, 1, f))


def _pad_cols(w, mult):
    n = w.shape[-1]
    pad = _round_up(n, mult) - n
    return jnp.pad(w, [(0, 0)] * (w.ndim - 1) + [(0, pad)]) if pad else w


def kernel(x_prompt, x_sample, state_rwkv, state_gla, c, c_ctx, w_ada, b_ada, norm_mix, norm_ffn, ffn_w_up, ffn_conv, ffn_w_down, norm_final, ev_w_in, ev_w_out, hy_short_w, hy_short_b, hy_w1, hy_b1, hy_w2, hy_b2, hy_w3, hy_freq, hy_bias, rw_mu, rw_w0, rw_w2, rw_a0, rw_a2, rw_g2, rw_kk, rw_ka, rw_rk, rw_ln_w, rw_ln_b, od_w_in, od_w_out, gla_gk2, gla_gk_b, gla_norm):
    bp, lp, d = x_prompt.shape
    bs, ls, _ = x_sample.shape
    geo = _Geo(bp, lp, bs, ls, d)
    depth = w_ada.shape[0]
    x = jnp.concatenate([x_prompt.reshape(geo.np_, d), x_sample.reshape(geo.ns, d)], axis=0)

    cond = jnp.concatenate([c_ctx[None, :], c], axis=0)
    cond = jnp.pad(cond, ((0, SUBLANES - cond.shape[0]), (0, 0)))
    mods_all = _adaln(cond, w_ada, b_ada)

    rw_states, gla_states = [], []
    for layer in range(depth):
        mods = mods_all[layer].reshape(SUBLANES * 6, 1, d)
        if layer % 2 == 0:
            e = layer // 2
            dh = hy_bias.shape[1]
            dr = rw_kk.shape[1]
            perm = _head_minor_perm(dr)
            perm3 = np.concatenate([perm, dr + perm, 2 * dr + perm, np.arange(3 * dr, ev_w_in.shape[2] - 3 * dh)])
            w_hy = ev_w_in[e][:, :3 * dh].astype(BF16)
            w_rw = _pad_cols(ev_w_in[e][:, 3 * dh:][:, perm3], 512).astype(BF16)
            p_hy = _norm_mm(geo, x, norm_mix[layer], mods, 0, w_hy)
            p_rw = _norm_mm(geo, x, norm_mix[layer], mods, 0, w_rw)
            x0, u = _hy_prep(geo, p_hy, hy_short_w[e], hy_short_b[e])
            filt_p = _hyena_filter(lp, hy_w1[e], hy_b1[e], hy_w2[e], hy_b2[e], hy_w3[e], hy_freq[e], dh)
            filt_s = _hyena_filter(ls, hy_w1[e], hy_b1[e], hy_w2[e], hy_b2[e], hy_w3[e], hy_freq[e], dh)
            y_hy = _hyena_long_conv(geo, x0, u, filt_p, filt_s, hy_bias[e])
            mu = _pad_cols(rw_mu[e][perm3], 512)
            y_rw, s_ctx = _rwkv_mixer(geo, p_rw, state_rwkv[:, e], mu, rw_w0[e][:, perm], rw_w2[e][:, :, perm],
                                      rw_a0[e][:, perm], rw_a2[e][:, :, perm], rw_g2[e][:, perm], rw_kk[e][perm],
                                      rw_ka[e][perm], rw_rk[e].reshape(dr)[perm], rw_ln_w[e][perm], rw_ln_b[e][perm])
            rw_states.append(s_ctx)
            mixed = jnp.concatenate([y_hy, y_rw], axis=-1)
            w_out = jnp.concatenate([ev_w_out[e][:dh], ev_w_out[e][dh:][perm]], axis=0).astype(BF16)
            x = _res_mm(geo, mixed, w_out, x, mods, 2)
        else:
            o = layer // 2
            dk = gla_gk2.shape[3]
            dv = gla_norm.shape[1] * GLA_H
            w_main = od_w_in[o][:, :2 * dk + 2 * dv].astype(BF16)
            w_gk = _pad_cols(od_w_in[o][:, 2 * dk + 2 * dv:], LANES).astype(BF16)
            p = _norm_mm(geo, x, norm_mix[layer], mods, 0, w_main)
            gkd = _norm_mm(geo, x, norm_mix[layer], mods, 0, w_gk)
            y, s_ctx = _gla_mixer(geo, p, gkd, state_gla[:, o], gla_gk2[o], gla_gk_b[o], gla_norm[o], dk, dv)
            gla_states.append(s_ctx)
            x = _res_mm(geo, y, od_w_out[o].astype(BF16), x, mods, 2)
        u_ffn = _norm_mm(geo, x, norm_ffn[layer], mods, 3, ffn_w_up[layer].astype(BF16),
                         tn_pref=512 if layer % 2 == 0 else 1024)
        act = _conv_gate(geo, u_ffn, ffn_conv[layer])
        x = _res_mm(geo, act, ffn_w_down[layer].astype(BF16), x, mods, 5)

    y = _final_norm(geo, x, norm_final)
    y_prompt = y[:geo.np_].reshape(bp, lp, d)
    y_sample = y[geo.np_:].reshape(bs, ls, d)
    return (y_prompt, y_sample, jnp.stack(rw_states, axis=1), jnp.stack(gla_states, axis=1))
```

```python
import functools
import math

import numpy as np
import jax
import jax.numpy as jnp
from jax import lax
from jax.experimental import pallas as pl
from jax.experimental.pallas import tpu as pltpu

F32 = jnp.float32
BF16 = jnp.bfloat16
HI = lax.Precision.HIGHEST

EPS = 1e-6
RW_HEAD = 64
RW_GN_EPS = 64e-5
GLA_H = 4
GLA_GATE_NORM = 16.0
GLA_CHUNK = 64
GRID_W = 64
HY_TARGET = 1e-2
HY_FAST = 0.3
HY_SLOW = 1.5

LANES = 128
SUBLANES = 8
MXU_DIM = 256
VMEM_LIMIT = 56 * 1024 * 1024
FFT_MINOR = 128


def _cparams(*sem):
    return pltpu.CompilerParams(dimension_semantics=sem, vmem_limit_bytes=VMEM_LIMIT)


def _round_up(n, m):
    return (n + m - 1) // m * m


def _pow2_tile(n, pref):
    t = 1
    while t * 2 <= pref and n % (t * 2) == 0:
        t *= 2
    return t


def _col_tile(n, pref):
    if n % LANES:
        return n
    divisors = [t for t in range(LANES, min(n, pref) + 1, LANES) if n % t == 0]
    mxu_wide = [t for t in divisors if t % MXU_DIM == 0]
    return max(mxu_wide or divisors)


class _Geo:
    def __init__(self, bp, lp, bs, ls, d):
        self.bp, self.lp, self.bs, self.ls, self.d = bp, lp, bs, ls, d
        self.np_, self.ns = bp * lp, bs * ls
        self.nt = self.np_ + self.ns
        self.tm = _pow2_tile(math.gcd(self.np_, ls), 512)
        assert lp & (lp - 1) == 0 and ls & (ls - 1) == 0, "sequence lengths must be powers of two"
        assert self.np_ % ls == 0, "sample rows must start on a sequence-length boundary"
        assert bs + 1 <= SUBLANES

    def row_tile(self, pref):
        return _pow2_tile(math.gcd(self.np_, self.ls), pref)

    def modrow(self, i, tm):
        r0 = i * tm
        return jnp.where(r0 < self.np_, 0, 1 + (r0 - self.np_) // self.ls)

    def seq_len(self, i, tm):
        return jnp.where(i * tm < self.np_, self.lp, self.ls)

    def mirror_tile(self, i, tm):
        def mirrored(tiles_per_seq):
            return (i // tiles_per_seq) * tiles_per_seq + (tiles_per_seq - 1 - i % tiles_per_seq)
        return jnp.where(i * tm < self.np_, mirrored(self.lp // tm), mirrored(self.ls // tm))

    @staticmethod
    def streams_per_group(dr):
        return LANES // (dr // RW_HEAD)

    def stream_slot(self, i, tm, direction, dr):
        r0 = i * tm
        batch = jnp.where(r0 < self.np_, r0 // self.lp, (r0 - self.np_) // self.ls)
        nbatch = jnp.where(r0 < self.np_, self.bp, self.bs)
        return (direction * nbatch + batch) % self.streams_per_group(dr)


def _adaln_kernel(c_ref, w_ref, b_ref, o_ref):
    c = c_ref[...]
    s = c * jax.nn.sigmoid(c)
    o_ref[0] = jnp.dot(s, w_ref[0], precision=HI, preferred_element_type=F32) + b_ref[0]


def _adaln(cond, w_ada, b_ada):
    depth, d, n = w_ada.shape
    tn = _col_tile(n, 1024)
    return pl.pallas_call(
        _adaln_kernel,
        grid=(depth, n // tn),
        in_specs=[pl.BlockSpec((SUBLANES, d), lambda l, j: (0, 0)),
                  pl.BlockSpec((1, d, tn), lambda l, j: (l, 0, j)),
                  pl.BlockSpec((1, 1, tn), lambda l, j: (l, 0, j))],
        out_specs=pl.BlockSpec((1, SUBLANES, tn), lambda l, j: (l, 0, j)),
        out_shape=jax.ShapeDtypeStruct((depth, SUBLANES, n), F32),
        compiler_params=_cparams("parallel", "arbitrary"),
        name="adaln",
    )(cond, w_ada, b_ada.reshape(depth, 1, n))


def _norm_mm_kernel(x_ref, g_ref, sh_ref, sc_ref, w_ref, o_ref, xn_ref):
    @pl.when(pl.program_id(1) == 0)
    def _():
        x = x_ref[...]
        ms = jnp.mean(x * x, axis=-1, keepdims=True)
        y = x * lax.rsqrt(ms + EPS) * g_ref[...]
        xn_ref[...] = (y * (1.0 + sc_ref[0]) + sh_ref[0]).astype(BF16)

    o_ref[...] = jnp.dot(xn_ref[...], w_ref[...], preferred_element_type=F32).astype(o_ref.dtype)


def _norm_mm(geo, x, g, mods, which_shift, w, out_dtype=F32, tm_pref=1024, tn_pref=1024):
    nt, d = x.shape
    n = w.shape[1]
    tm = geo.row_tile(tm_pref)
    tn = _col_tile(n, tn_pref)

    def mod_map(which):
        return lambda i, j: (geo.modrow(i, tm) * 6 + which, 0, 0)

    return pl.pallas_call(
        _norm_mm_kernel,
        grid=(nt // tm, n // tn),
        in_specs=[pl.BlockSpec((tm, d), lambda i, j: (i, 0)),
                  pl.BlockSpec((1, d), lambda i, j: (0, 0)),
                  pl.BlockSpec((1, 1, d), mod_map(which_shift)),
                  pl.BlockSpec((1, 1, d), mod_map(which_shift + 1)),
                  pl.BlockSpec((d, tn), lambda i, j: (0, j))],
        out_specs=pl.BlockSpec((tm, tn), lambda i, j: (i, j)),
        out_shape=jax.ShapeDtypeStruct((nt, n), out_dtype),
        scratch_shapes=[pltpu.VMEM((tm, d), BF16)],
        compiler_params=_cparams("parallel", "arbitrary"),
        name="norm_mm",
    )(x, g.reshape(1, d), mods, mods, w)


def _res_mm_kernel(a_ref, w_ref, res_ref, gt_ref, o_ref):
    acc = jnp.dot(a_ref[...], w_ref[...], preferred_element_type=F32)
    o_ref[...] = res_ref[...] + gt_ref[0] * acc


def _res_mm(geo, a, w, res, mods, which_gate, tm_pref=1024):
    nt, k = a.shape
    n = w.shape[1]
    tm = geo.row_tile(tm_pref)
    tn = _col_tile(n, 1024 if k <= 2048 else 512)
    return pl.pallas_call(
        _res_mm_kernel,
        grid=(nt // tm, n // tn),
        in_specs=[pl.BlockSpec((tm, k), lambda i, j: (i, 0)),
                  pl.BlockSpec((k, tn), lambda i, j: (0, j)),
                  pl.BlockSpec((tm, tn), lambda i, j: (i, j)),
                  pl.BlockSpec((1, 1, tn), lambda i, j: (geo.modrow(i, tm) * 6 + which_gate, 0, j))],
        out_specs=pl.BlockSpec((tm, tn), lambda i, j: (i, j)),
        out_shape=jax.ShapeDtypeStruct((nt, n), F32),
        compiler_params=_cparams("parallel", "arbitrary"),
        name="res_mm",
    )(a, w, res, mods)


def _final_norm_kernel(x_ref, g_ref, o_ref):
    x = x_ref[...]
    ms = jnp.mean(x * x, axis=-1, keepdims=True)
    o_ref[...] = x * lax.rsqrt(ms + EPS) * g_ref[...]


def _final_norm(geo, x, g):
    nt, d = x.shape
    tm = geo.tm
    return pl.pallas_call(
        _final_norm_kernel,
        grid=(nt // tm,),
        in_specs=[pl.BlockSpec((tm, d), lambda i: (i, 0)), pl.BlockSpec((1, d), lambda i: (0, 0))],
        out_specs=pl.BlockSpec((tm, d), lambda i: (i, 0)),
        out_shape=jax.ShapeDtypeStruct((nt, d), F32),
        compiler_params=_cparams("parallel"),
        name="final_norm",
    )(x, g.reshape(1, d))


def _seq_pos(geo, i, tm):
    lseq = geo.seq_len(i, tm)
    rid = lax.broadcasted_iota(jnp.int32, (tm, 1), 0) + i * tm
    return rid & (lseq - 1), lseq


def _neighbours(x, prev_row, next_row, tpos, lseq):
    tm = x.shape[0]
    rid = lax.broadcasted_iota(jnp.int32, (tm, 1), 0)
    xm = pltpu.roll(x, 1, axis=0)
    xm = jnp.where(rid == 0, prev_row, xm)
    xm = jnp.where(tpos == 0, 0.0, xm)
    xp = pltpu.roll(x, tm - 1, axis=0)
    xp = jnp.where(rid == tm - 1, next_row, xp)
    xp = jnp.where(tpos == lseq - 1, 0.0, xp)
    return xm, xp


def _halo_specs(nt, tm, width, col):
    r = tm // SUBLANES
    last = nt // SUBLANES - 1
    prev = pl.BlockSpec((SUBLANES, width), lambda i, j: (jnp.maximum(i * r - 1, 0), col(j)))
    nxt = pl.BlockSpec((SUBLANES, width), lambda i, j: (jnp.minimum((i + 1) * r, last), col(j)))
    return prev, nxt


def _hy_prep_kernel(geo, tm, *refs):
    (x0_ref, x0p_ref, x0n_ref, x1_ref, x1p_ref, x1n_ref, x2_ref, x2p_ref, x2n_ref,
     w_ref, b_ref, o0_ref, u_ref) = refs
    i = pl.program_id(0)
    tpos, lseq = _seq_pos(geo, i, tm)

    def conv(x_ref, p_ref, n_ref, g):
        x = x_ref[...]
        xm, xp = _neighbours(x, p_ref[SUBLANES - 1:SUBLANES, :], n_ref[0:1, :], tpos, lseq)
        return xm * w_ref[0, g] + x * w_ref[1, g] + xp * w_ref[2, g] + b_ref[g]

    c0 = conv(x0_ref, x0p_ref, x0n_ref, 0)
    c1 = conv(x1_ref, x1p_ref, x1n_ref, 1)
    c2 = conv(x2_ref, x2p_ref, x2n_ref, 2)
    o0_ref[...] = c0
    u_ref[...] = c1 * c2


def _hy_prep(geo, p_hy, short_w, short_b):
    nt = p_hy.shape[0]
    dh = p_hy.shape[1] // 3
    tm = _pow2_tile(geo.tm, 256)
    tc = _col_tile(dh, 512)
    ncb = dh // tc
    specs = []
    for g in range(3):
        col = (lambda j, g=g: g * ncb + j)
        specs.append(pl.BlockSpec((tm, tc), lambda i, j, col=col: (i, col(j))))
        specs.extend(_halo_specs(nt, tm, tc, col))
    specs.append(pl.BlockSpec((3, 3, 1, tc), lambda i, j: (0, 0, 0, j)))
    specs.append(pl.BlockSpec((3, 1, tc), lambda i, j: (0, 0, j)))
    out_spec = pl.BlockSpec((tm, tc), lambda i, j: (i, j))
    w4 = short_w.reshape(3, 3, 1, dh)
    b3 = short_b.reshape(3, 1, dh)
    return pl.pallas_call(
        functools.partial(_hy_prep_kernel, geo, tm),
        grid=(nt // tm, ncb),
        in_specs=specs,
        out_specs=[out_spec, out_spec],
        out_shape=[jax.ShapeDtypeStruct((nt, dh), F32)] * 2,
        compiler_params=_cparams("parallel", "arbitrary"),
        name="hy_prep",
    )(*([p_hy] * 9), w4, b3)


def _lmm_kernel(has_epi, *refs):
    if has_epi:
        a_ref, x_ref, x0_ref, u_ref, bias_ref, o_ref = refs
    else:
        a_ref, x_ref, o_ref = refs
    acc = jnp.dot(a_ref[0], x_ref[0], precision=HI, preferred_element_type=F32)
    if has_epi:
        acc = x0_ref[0] * (acc + u_ref[0] * bias_ref[...])
    o_ref[0] = acc.astype(o_ref.dtype)


def _lmm(a, x, epi=None, out_dtype=F32):
    gm, m, k = a.shape
    g, _, n = x.shape
    tn = _col_tile(n, 2048)
    xspec = pl.BlockSpec((1, k, tn), lambda gi, j: (gi, 0, j))
    ospec = pl.BlockSpec((1, m, tn), lambda gi, j: (gi, 0, j))
    specs = [pl.BlockSpec((1, m, k), lambda gi, j: (gi % gm, 0, 0)), xspec]
    args = [a, x]
    if epi is not None:
        x0, u, bias = epi
        specs += [ospec, ospec, pl.BlockSpec((1, tn), lambda gi, j: (0, j))]
        args += [x0, u, bias]
    return pl.pallas_call(
        functools.partial(_lmm_kernel, epi is not None),
        grid=(g, n // tn),
        in_specs=specs,
        out_specs=ospec,
        out_shape=jax.ShapeDtypeStruct((g, m, n), out_dtype),
        compiler_params=_cparams("parallel", "arbitrary"),
        name="dft_lmm",
    )(*args)


def _spec_kernel(has_epi, *refs):
    if has_epi:
        mf_ref, h_ref, mi_ref, x_ref, x0_ref, bias_ref, o_ref = refs
    else:
        mf_ref, h_ref, mi_ref, x_ref, o_ref = refs
    x = x_ref[0]
    f = jnp.dot(mf_ref[0], x, precision=HI, preferred_element_type=F32)
    r = f.shape[0] // 2
    fr, fi = f[:r], f[r:]
    hr, hi = h_ref[0, :r], h_ref[0, r:]
    y = jnp.concatenate([fr * hr - fi * hi, fr * hi + fi * hr], axis=0)
    out = jnp.dot(mi_ref[0], y, precision=HI, preferred_element_type=F32)
    if has_epi:
        out = x0_ref[0] * (out + x * bias_ref[...])
    o_ref[0] = out.astype(o_ref.dtype)


def _spectral(mf, h, mi, x, epi=None, out_dtype=F32):
    gm, r2, k = mf.shape
    kout = mi.shape[1]
    g, _, c = x.shape
    tc = _col_tile(c, 512)
    xspec = pl.BlockSpec((1, k, tc), lambda gi, j: (gi, 0, j))
    ospec = pl.BlockSpec((1, kout, tc), lambda gi, j: (gi, 0, j))
    specs = [pl.BlockSpec((1, r2, k), lambda gi, j: (gi % gm, 0, 0)),
             pl.BlockSpec((1, r2, tc), lambda gi, j: (gi % gm, 0, j)),
             pl.BlockSpec((1, kout, r2), lambda gi, j: (gi % gm, 0, 0)),
             xspec]
    args = [mf, h, mi, x]
    if epi is not None:
        x0, bias = epi
        specs += [ospec, pl.BlockSpec((1, tc), lambda gi, j: (0, j))]
        args += [x0, bias]
    return pl.pallas_call(
        functools.partial(_spec_kernel, epi is not None),
        grid=(g, c // tc),
        in_specs=specs,
        out_specs=ospec,
        out_shape=jax.ShapeDtypeStruct((g, kout, c), out_dtype),
        compiler_params=_cparams("parallel", "arbitrary"),
        name="dft_spectral",
    )(*args)


def _dft_tables_direct(l):
    n = 2 * l
    nf = l + 1
    r = _round_up(nf, SUBLANES)
    kk = np.arange(nf)[:, None].astype(np.float64)
    ang_half = 2.0 * np.pi * ((kk * np.arange(l)[None, :]) % n) / n
    ang_full = 2.0 * np.pi * ((kk * np.arange(n)[None, :]) % n) / n

    def fwd(ang):
        m = np.zeros((2 * r, ang.shape[1]))
        m[:nf] = np.cos(ang)
        m[r:r + nf] = -np.sin(ang)
        return m

    coef = np.full((nf,), 2.0 / n)
    coef[0] = coef[-1] = 1.0 / n
    inv = np.zeros((l, 2 * r))
    inv[:, :nf] = (np.cos(ang_half) * coef[:, None]).T
    inv[:, r:r + nf] = (-np.sin(ang_half) * coef[:, None]).T
    f32 = lambda a: jnp.asarray(a[None], F32)
    return f32(fwd(ang_half)), f32(fwd(ang_full)), f32(inv)


def _dft_tables_two_level(l):
    n = 2 * l
    n1 = FFT_MINOR
    n2 = n // n1
    nb = n2 // 2 + 1
    nbp = _round_up(nb, 4)
    k2 = np.arange(nb)[:, None].astype(np.float64)
    ang2 = 2.0 * np.pi * ((k2 * np.arange(n2)[None, :]) % n2) / n2
    ms_full = np.zeros((2 * nbp, n2))
    ms_full[0:2 * nb:2] = np.cos(ang2)
    ms_full[1:2 * nb:2] = -np.sin(ang2)
    ms_half = ms_full[:, :n2 // 2]
    coef = np.full((nb,), 2.0 / n)
    coef[0] = coef[-1] = 1.0 / n
    ms_inv = np.zeros((n2 // 2, 2 * nbp))
    ms_inv[:, 0:2 * nb:2] = (np.cos(ang2[:, :n2 // 2]) * coef[:, None]).T
    ms_inv[:, 1:2 * nb:2] = (-np.sin(ang2[:, :n2 // 2]) * coef[:, None]).T

    a_fwd = np.zeros((nbp, 2 * n1, 2 * n1))
    a_inv = np.zeros((nbp, 2 * n1, 2 * n1))
    i1 = np.arange(n1).astype(np.float64)
    for b in range(nb):
        ph = 2.0 * np.pi * (((np.outer(i1, i1) * n2) + (i1[None, :] * b)) % n) / n
        wr, wi = np.cos(ph), -np.sin(ph)
        a_fwd[b] = np.block([[wr, -wi], [wi, wr]])
        ph = 2.0 * np.pi * (((np.outer(i1, i1) * n2) + (i1[:, None] * b)) % n) / n
        pr, pi_ = np.cos(ph), np.sin(ph)
        a_inv[b] = np.block([[pr, -pi_], [pi_, pr]])
    f32 = lambda a: jnp.asarray(a, F32)
    return (f32(ms_half[None]), f32(ms_full[None]), f32(ms_inv[None]), f32(a_fwd), f32(a_inv), n1, n2, nbp)


def _hyena_filter(l, w1, b1, w2, b2, w3, freq, dh):
    pos = jnp.arange(l, dtype=F32)[:, None]
    t = pos / max(l - 1, 1)
    emb = w1.shape[0]
    bands = (emb - 1) // 2
    fb = jnp.linspace(1e-4, bands - 1, bands, dtype=F32)
    ang = (2.0 * math.pi / l) * pos * fb
    z = jnp.concatenate([t, jnp.cos(ang), -jnp.sin(ang)], axis=-1)
    h = jnp.sin(freq[0] * (jnp.dot(z, w1, precision=HI) + b1))
    h = jnp.sin(freq[1] * (jnp.dot(h, w2, precision=HI) + b2))
    h = jnp.dot(h, w3, precision=HI).reshape(l, 2, dh)
    deltas = jnp.abs(jnp.linspace(math.log(HY_TARGET) / HY_SLOW, math.log(HY_TARGET) / HY_FAST, dh, dtype=F32))
    h = h * jnp.exp(-t[:, :, None] * deltas)
    filt = jnp.concatenate([h[:, 0], jnp.zeros((1, dh), F32), jnp.flip(h[1:, 1], axis=0)], axis=0)
    return filt / jnp.sum(jnp.abs(filt), axis=0, keepdims=True)


def _hyena_long_conv(geo, x0, u, filt_p, filt_s, bias):
    dh = u.shape[1]
    bias2 = bias.reshape(1, dh)
    mf_half, mf_full, mi = _dft_tables_direct(geo.lp)
    h_p = _lmm(mf_full, filt_p[None])
    up = u[:geo.np_].reshape(geo.bp, geo.lp, dh)
    x0p = x0[:geo.np_].reshape(geo.bp, geo.lp, dh)
    y_p = _spectral(mf_half, h_p, mi, up, epi=(x0p, bias2), out_dtype=BF16)
    ms_half, ms_full, ms_inv, a_fwd, a_inv, n1, n2, nbp = _dft_tables_two_level(geo.ls)
    hs = _lmm(ms_full, filt_s.reshape(1, n2, n1 * dh))
    hs = _lmm(a_fwd, hs.reshape(nbp, 2 * n1, dh))
    us = u[geo.np_:].reshape(geo.bs, n2 // 2, n1 * dh)
    x0s = x0[geo.np_:].reshape(geo.bs, n2 // 2, n1 * dh)
    xs = _lmm(ms_half, us)
    zs = _spectral(a_fwd, hs, a_inv, xs.reshape(geo.bs * nbp, 2 * n1, dh))
    zs = zs.reshape(geo.bs, 2 * nbp, n1 * dh)
    y_s = _lmm(ms_inv, zs, epi=(x0s, us, jnp.tile(bias2, (1, n1))), out_dtype=BF16)
    return jnp.concatenate([y_p.reshape(geo.np_, dh), y_s.reshape(geo.ns, dh)], axis=0)


def _head_sum(x):
    dr = x.shape[1]
    nheads = dr // RW_HEAD
    assert LANES % nheads == 0
    t = x[:, 0:LANES]
    for q in range(1, dr // LANES):
        t = t + x[:, q * LANES:(q + 1) * LANES]
    shift = LANES // 2
    while shift >= nheads:
        t = t + pltpu.roll(t, shift, axis=1)
        shift //= 2
    return jnp.concatenate([t] * (dr // LANES), axis=1)


def _flip_rows(x):
    tm, c = x.shape
    x3 = x.reshape(tm // SUBLANES, SUBLANES, c)
    rid = lax.broadcasted_iota(jnp.int32, (1, SUBLANES, 1), 1)
    for s in (4, 2, 1):
        x3 = jnp.where((rid & s) != 0, pltpu.roll(x3, s, axis=1), pltpu.roll(x3, SUBLANES - s, axis=1))
    return jnp.concatenate([x3[j] for j in reversed(range(tm // SUBLANES))], axis=0)


def _rw_prep_kernel(geo, tm, dr, *refs):
    (x_ref, xp_ref, xn_ref, mu_ref, w0_ref, w2_ref, a0_ref, a2_ref, g2_ref, kk_ref, ka_ref, rk_ref,
     rf_o, nkkf_o, vf_o, wf_o, kxf_o, bf_o, rb_o, nkkb_o, vb_o, wb_o, kxb_o, bb_o, g_o, bon_o) = refs
    i = pl.program_id(0)
    tpos, lseq = _seq_pos(geo, i, tm)
    x = x_ref[...]
    xm, xp = _neighbours(x, xp_ref[SUBLANES - 1:SUBLANES, :], xn_ref[0:1, :], tpos, lseq)
    x = x + mu_ref[...] * (0.5 * (xm + xp) - x)
    r = x[:, 0:dr]
    k = x[:, dr:2 * dr]
    v = x[:, 2 * dr:3 * dr]
    wd = x[:, 3 * dr:3 * dr + LANES]
    ad = x[:, 3 * dr + LANES:3 * dr + 2 * LANES]
    gd = x[:, 3 * dr + 2 * LANES:3 * dr + 3 * LANES]
    wl = w0_ref[...] + jnp.dot(jnp.tanh(wd), w2_ref[...], precision=HI, preferred_element_type=F32)
    w_log = -jax.nn.softplus(-wl) - 0.5
    decay = jnp.exp(-jnp.exp(w_log))
    a = jax.nn.sigmoid(a0_ref[...] + jnp.dot(ad, a2_ref[...], precision=HI, preferred_element_type=F32))
    g_o[...] = jnp.dot(jax.nn.sigmoid(gd), g2_ref[...], precision=HI, preferred_element_type=F32)
    kk = k * kk_ref[...]
    kk = kk * lax.rsqrt(_head_sum(kk * kk) + 1e-12)
    a_0, a_1 = a[:, :dr], a[:, dr:]
    kx0 = k * (1.0 + (a_0 - 1.0) * ka_ref[...])
    kx1 = k * (1.0 + (a_1 - 1.0) * ka_ref[...])
    bon_o[...] = _head_sum(r * (kx0 + kx1) * rk_ref[...]) * v

    slab = LANES // geo.streams_per_group(dr)

    def emit(o_ref, val, direction):
        if direction:
            val = _flip_rows(val)
        shift = geo.stream_slot(i, tm, direction, dr) * slab
        for q in range(dr // LANES):
            o_ref[:, q * LANES:(q + 1) * LANES] = pltpu.roll(val[:, q * LANES:(q + 1) * LANES], shift, axis=1)

    nkk = -kk
    for o_f, o_b, val_f, val_b in ((rf_o, rb_o, r, r), (nkkf_o, nkkb_o, nkk, nkk), (vf_o, vb_o, v, v),
                                   (wf_o, wb_o, decay[:, :dr], decay[:, dr:]), (kxf_o, kxb_o, kx0, kx1),
                                   (bf_o, bb_o, kk * a_0, kk * a_1)):
        emit(o_f, val_f, 0)
        emit(o_b, val_b, 1)


def _blockdiag2(w):
    _, r, c = w.shape
    z = jnp.zeros((r, c), w.dtype)
    return jnp.concatenate([jnp.concatenate([w[0], z], axis=1), jnp.concatenate([z, w[1]], axis=1)], axis=0)


def _head_minor_perm(dr):
    return np.arange(dr).reshape(dr // RW_HEAD, RW_HEAD).T.reshape(-1)


def _rw_prep(geo, p_rw, mu, w0, w2, a0, a2, g2, k_k, k_a, r_k):
    nt, width = p_rw.shape
    dr = k_k.shape[0]
    assert 2 * w2.shape[1] == LANES and 2 * a2.shape[1] == LANES and g2.shape[0] == LANES
    tm = _pow2_tile(min(geo.tm, geo.lp), 128)
    full = lambda shape: pl.BlockSpec(shape, lambda i, j: tuple(0 for _ in shape))
    prev, nxt = _halo_specs(nt, tm, width, lambda j: 0)
    consts = [mu.reshape(1, width), w0.reshape(1, 2 * dr), _blockdiag2(w2), a0.reshape(1, 2 * dr),
              _blockdiag2(a2), g2, k_k.reshape(1, dr), k_a.reshape(1, dr), r_k.reshape(1, dr)]
    ospec = pl.BlockSpec((tm, dr), lambda i, j: (i, 0))
    mspec = pl.BlockSpec((tm, dr), lambda i, j: (geo.mirror_tile(i, tm), 0))
    return pl.pallas_call(
        functools.partial(_rw_prep_kernel, geo, tm, dr),
        grid=(nt // tm, 1),
        in_specs=[pl.BlockSpec((tm, width), lambda i, j: (i, 0)), prev, nxt] + [full(c.shape) for c in consts],
        out_specs=[ospec] * 6 + [mspec] * 6 + [ospec] * 2,
        out_shape=[jax.ShapeDtypeStruct((nt, dr), F32)] * 14,
        compiler_params=_cparams("parallel", "arbitrary"),
        name="rw_prep",
    )(p_rw, p_rw, p_rw, *consts)


def _rw_scan_kernel(tc, r_ref, nkk_ref, v_ref, w_ref, kx_ref, b_ref, s0_ref, y_ref, so_ref, s_ref):
    t_chunk = pl.program_id(1)

    @pl.when(t_chunk == 0)
    def _():
        s_ref[...] = s0_ref[...]

    nacc = 2
    vhalf = RW_HEAD // 2

    def row(ref, t, k):
        return ref[t // SUBLANES, k, pl.ds(t % SUBLANES, 1), :]

    def state_dot_neg_kk(t):
        acc = [jnp.zeros((RW_HEAD, LANES), F32) for _ in range(nacc)]
        for k in range(RW_HEAD):
            acc[k % nacc] = acc[k % nacc] + s_ref[k] * row(nkk_ref, t, k)
        return acc[0] + acc[1]

    def step(t, sa):
        t_next = jnp.minimum(t + 1, tc - 1)
        sa_next = []
        for half in range(2):
            vs = slice(half * vhalf, (half + 1) * vhalf)
            sa_h = sa[vs]
            vv = v_ref[t, vs, :]
            acc_y = [jnp.zeros((vhalf, LANES), F32) for _ in range(nacc)]
            acc_s = [jnp.zeros((vhalf, LANES), F32) for _ in range(nacc)]
            for k in range(RW_HEAD):
                new = s_ref[k, vs, :] * row(w_ref, t, k) + sa_h * row(b_ref, t, k) + vv * row(kx_ref, t, k)
                s_ref[k, vs, :] = new
                acc_y[k % nacc] = acc_y[k % nacc] + new * row(r_ref, t, k)
                acc_s[k % nacc] = acc_s[k % nacc] + new * row(nkk_ref, t_next, k)
            y_ref[t, vs, :] = acc_y[0] + acc_y[1]
            sa_next.append(acc_s[0] + acc_s[1])
        return jnp.concatenate(sa_next, axis=0)

    lax.fori_loop(0, tc, step, state_dot_neg_kk(0))

    @pl.when(t_chunk == pl.num_programs(1) - 1)
    def _():
        so_ref[...] = s_ref[...]


def _rw_scan(r, nkk, v, w, kx, b, s0):
    t_len, n, lanes = v.shape
    tc = _pow2_tile(t_len, 32)
    xspec = pl.BlockSpec((tc, n, LANES), lambda g, t: (t, 0, g))
    rspec = pl.BlockSpec((tc // SUBLANES, n, SUBLANES, LANES), lambda g, t: (t, 0, 0, g))
    sspec = pl.BlockSpec((n, n, LANES), lambda g, t: (0, 0, g))
    return pl.pallas_call(
        functools.partial(_rw_scan_kernel, tc),
        grid=(lanes // LANES, t_len // tc),
        in_specs=[rspec, rspec, xspec, rspec, rspec, rspec, sspec],
        out_specs=[xspec, sspec],
        out_shape=[jax.ShapeDtypeStruct((t_len, n, lanes), F32), jax.ShapeDtypeStruct((n, n, lanes), F32)],
        scratch_shapes=[pltpu.VMEM((n, n, LANES), F32)],
        compiler_params=_cparams("parallel", "arbitrary"),
        name="rw_scan",
    )(r, nkk, v, w, kx, b, s0)


def _merge_slabs(pieces, slab, offset):
    n = len(pieces)
    lane_slab = lax.broadcasted_iota(jnp.int32, pieces[0].shape, 1) // slab
    acc = pieces[n - 1]
    for p in range(n - 1):
        acc = jnp.where(lane_slab == (offset + p) % n, pieces[p], acc)
    return acc


def _to_streams_kernel(nsrc, ntens, slab, value_index, *refs):
    srcs, outs = refs[:ntens * nsrc], refs[ntens * nsrc:]
    for ti in range(ntens):
        tiles = [srcs[ti * nsrc + s][...] for s in range(nsrc)]
        tt = tiles[0].shape[0]
        for j in range(SUBLANES):
            merged = _merge_slabs(tiles, slab, j)
            shift = ((nsrc - j) % nsrc) * slab
            merged = pltpu.roll(merged, shift, axis=1) if shift else merged
            if ti == value_index:
                outs[ti][:, j, :] = merged
            else:
                outs[ti][:, j] = merged.reshape(tt // SUBLANES, SUBLANES, LANES)


def _to_streams(fwd, bwd, row0, nbatch, l, direction):
    ntens = len(fwd)
    dr = fwd[0].shape[1]
    nsrc = _Geo.streams_per_group(dr)
    assert nsrc == SUBLANES, "the stream layout assumes 16 heads (eight channels per 128-lane tile)"
    slab = LANES // nsrc
    tt = _pow2_tile(l, 128)
    if direction is None:
        assert 2 * nbatch == nsrc
        ngroups = 1
        source = lambda g, s: (s // nbatch, s % nbatch)
    else:
        assert nbatch % nsrc == 0
        ngroups = nbatch // nsrc
        source = lambda g, s: (direction, g * nsrc + s)
    specs, args = [], []
    for ti in range(ntens):
        for s in range(nsrc):
            d = source(0, s)[0]
            specs.append(pl.BlockSpec(
                (tt, LANES), lambda g, i, q, s=s: ((row0 + source(g, s)[1] * l) // tt + i, q)))
            args.append((bwd if d else fwd)[ti])
    value_index = 2
    vspec = pl.BlockSpec((tt, SUBLANES, LANES), lambda g, i, q: (i, q, g))
    rspec = pl.BlockSpec((tt // SUBLANES, SUBLANES, SUBLANES, LANES), lambda g, i, q: (i, q, 0, g))
    vshape = jax.ShapeDtypeStruct((l, RW_HEAD, ngroups * LANES), F32)
    rshape = jax.ShapeDtypeStruct((l // SUBLANES, RW_HEAD, SUBLANES, ngroups * LANES), F32)
    return pl.pallas_call(
        functools.partial(_to_streams_kernel, nsrc, ntens, slab, value_index),
        grid=(ngroups, l // tt, dr // LANES),
        in_specs=specs,
        out_specs=[vspec if ti == value_index else rspec for ti in range(ntens)],
        out_shape=[vshape if ti == value_index else rshape for ti in range(ntens)],
        compiler_params=_cparams("parallel", "parallel", "arbitrary"),
        name="to_streams",
    )(*args)


def _from_streams_kernel(nsrc, slab, y_ref, o_ref):
    rolled = [y_ref[:, j, :] if j == 0 else pltpu.roll(y_ref[:, j, :], j * slab, axis=1) for j in range(SUBLANES)]
    for s in range(nsrc):
        merged = _merge_slabs(rolled, slab, s)
        shift = ((nsrc - s) % nsrc) * slab
        o_ref[s] = pltpu.roll(merged, shift, axis=1) if shift else merged


def _from_streams(y, dr):
    l, _, lanes = y.shape
    nsrc = _Geo.streams_per_group(dr)
    slab = LANES // nsrc
    tt = _pow2_tile(l, 128)
    ngroups = lanes // LANES
    return pl.pallas_call(
        functools.partial(_from_streams_kernel, nsrc, slab),
        grid=(ngroups, l // tt, dr // LANES),
        in_specs=[pl.BlockSpec((tt, SUBLANES, LANES), lambda g, i, q: (i, q, g))],
        out_specs=pl.BlockSpec((nsrc, tt, LANES), lambda g, i, q: (g, i, q)),
        out_shape=jax.ShapeDtypeStruct((ngroups * nsrc, l, dr), F32),
        compiler_params=_cparams("parallel", "parallel", "arbitrary"),
        name="from_streams",
    )(y)


def _rw_post_kernel(yf_ref, yb_ref, bon_ref, g_ref, lnw_ref, lnb_ref, o_ref):
    y = yf_ref[...] + _flip_rows(yb_ref[...])
    inv_n = 1.0 / RW_HEAD
    mean = _head_sum(y) * inv_n
    yc = y - mean
    var = _head_sum(yc * yc) * inv_n
    yn = yc * lax.rsqrt(var + RW_GN_EPS) * lnw_ref[...] + lnb_ref[...]
    o_ref[...] = ((yn + bon_ref[...]) * g_ref[...]).astype(o_ref.dtype)


def _rw_post(geo, yf, yb, bonus, g, ln_w, ln_b):
    nt, dr = yf.shape
    tm = _pow2_tile(min(geo.tm, geo.lp), 256)
    tspec = pl.BlockSpec((tm, dr), lambda i: (i, 0))
    mspec = pl.BlockSpec((tm, dr), lambda i: (geo.mirror_tile(i, tm), 0))
    full = lambda a: pl.BlockSpec(a.shape, lambda i: (0, 0))
    consts = [ln_w.reshape(1, dr), ln_b.reshape(1, dr)]
    return pl.pallas_call(
        _rw_post_kernel,
        grid=(nt // tm,),
        in_specs=[tspec, mspec, tspec, tspec] + [full(c) for c in consts],
        out_specs=tspec,
        out_shape=jax.ShapeDtypeStruct((nt, dr), BF16),
        compiler_params=_cparams("parallel"),
        name="rw_post",
    )(yf, yb, bonus, g, *consts)


def _rwkv_mixer(geo, p_rw, s0_sample, mu, w0, w2, a0, a2, g2, k_k, k_a, r_k, ln_w, ln_b):
    dr = k_k.shape[0]
    h = dr // RW_HEAD
    outs = _rw_prep(geo, p_rw, mu, w0, w2, a0, a2, g2, k_k, k_a, r_k)
    fwd, bwd, (g, bonus) = outs[0:6], outs[6:12], outs[12:14]
    s0 = s0_sample.transpose(4, 3, 1, 0, 2).reshape(RW_HEAD, RW_HEAD, 2 * geo.bs * h)
    y_s, _ = _rw_scan(*_to_streams(fwd, bwd, geo.np_, geo.bs, geo.ls, None), s0)
    y_s = _from_streams(y_s, dr)
    y_p, s_p = [], []
    for direction in range(2):
        zero = jnp.zeros((RW_HEAD, RW_HEAD, geo.bp * h), F32)
        y_d, s_d = _rw_scan(*_to_streams(fwd, bwd, 0, geo.bp, geo.lp, direction), zero)
        y_p.append(_from_streams(y_d, dr))
        s_p.append(s_d.reshape(RW_HEAD, RW_HEAD, geo.bp, h))
    states = jnp.stack(s_p).transpose(3, 0, 4, 2, 1)
    yf = jnp.concatenate([y_p[0].reshape(geo.np_, dr), y_s[:geo.bs].reshape(geo.ns, dr)], axis=0)
    yb = jnp.concatenate([y_p[1].reshape(geo.np_, dr), y_s[geo.bs:].reshape(geo.ns, dr)], axis=0)
    return _rw_post(geo, yf, yb, bonus, g, ln_w, ln_b), states


def _gla_dir(q, k, v, gkd, gk2, gkb, s_ref, o_ref, reverse):
    c, dk = q.shape
    hk, hv = dk // GLA_H, v.shape[1] // GLA_H
    z = jnp.dot(gkd, gk2, precision=HI, preferred_element_type=F32) + gkb
    lg = (jnp.minimum(z, 0.0) - jnp.log1p(jnp.exp(-jnp.abs(z)))) * (1.0 / GLA_GATE_NORM)
    row = lax.broadcasted_iota(jnp.int32, (c, c), 0)
    col = lax.broadcasted_iota(jnp.int32, (c, c), 1)
    tri = (col >= row) if reverse else (col <= row)
    b = jnp.dot(tri.astype(F32), lg, precision=HI, preferred_element_type=F32)
    i_ref = c // 2 if reverse else c // 2 - 1
    i_last = 0 if reverse else c - 1
    b_ref = b[i_ref:i_ref + 1, :]
    q_in = (q * jnp.exp(b - b_ref)).astype(BF16)
    k_in = (k * jnp.exp(b_ref - b)).astype(BF16)
    q_dec = (q * jnp.exp(b)).astype(BF16)
    bt = b.T
    bt_last = bt[:, i_last:i_last + 1]
    k_dec_t = (k.T * jnp.exp(bt_last - bt)).astype(BF16)
    chunk_decay = jnp.exp(bt_last)
    vb = v.astype(BF16)
    for h in range(GLA_H):
        ks = slice(h * hk, (h + 1) * hk)
        vs = slice(h * hv, (h + 1) * hv)
        s = s_ref[h]
        scores = lax.dot_general(q_in[:, ks], k_in[:, ks], (((1,), (1,)), ((), ())), preferred_element_type=F32)
        scores = jnp.where(tri, scores, 0.0).astype(BF16)
        o_ref[:, vs] = (jnp.dot(scores, vb[:, vs], preferred_element_type=F32)
                        + jnp.dot(q_dec[:, ks], s.astype(BF16), preferred_element_type=F32))
        s_ref[h] = s * chunk_decay[ks] + jnp.dot(k_dec_t[ks], vb[:, vs], preferred_element_type=F32)


def _gla_kernel(scale, *refs):
    (qf_ref, kf_ref, vf_ref, df_ref, qb_ref, kb_ref, vb_ref, db_ref, gk2_ref, gkb_ref, s0_ref,
     of_ref, ob_ref, so_ref, sf, sb) = refs
    ci = pl.program_id(1)

    @pl.when(ci == 0)
    def _():
        sf[...] = s0_ref[0, 0]
        sb[...] = s0_ref[0, 1]

    rank = gk2_ref.shape[1]
    _gla_dir(qf_ref[...] * scale, kf_ref[...], vf_ref[...], df_ref[:, 0:rank], gk2_ref[0], gkb_ref[0],
             sf, of_ref, False)
    _gla_dir(qb_ref[...] * scale, kb_ref[...], vb_ref[...], db_ref[:, rank:2 * rank], gk2_ref[1], gkb_ref[1],
             sb, ob_ref, True)

    @pl.when(ci == pl.num_programs(1) - 1)
    def _():
        so_ref[0, 0] = sf[...]
        so_ref[0, 1] = sb[...]


def _gla_scan(p, gkd, row0, b, l, dk, dv, gk2, gk_b, s0):
    hk, hv = dk // GLA_H, dv // GLA_H
    c = GLA_CHUNK
    n = l // c
    blk0 = row0 // c
    rank = gk2.shape[1]
    assert (2 * dk) % dv == 0

    def tok(bi, ci, rev):
        return blk0 + bi * n + (n - 1 - ci if rev else ci)

    def specs(rev):
        return [pl.BlockSpec((c, dk), lambda bi, ci: (tok(bi, ci, rev), 0)),
                pl.BlockSpec((c, dk), lambda bi, ci: (tok(bi, ci, rev), 1)),
                pl.BlockSpec((c, dv), lambda bi, ci: (tok(bi, ci, rev), 2 * dk // dv)),
                pl.BlockSpec((c, LANES), lambda bi, ci: (tok(bi, ci, rev), 0))]

    def ospec(rev):
        return pl.BlockSpec((c, dv), lambda bi, ci: (bi * n + (n - 1 - ci if rev else ci), 0))

    sspec = pl.BlockSpec((1, 2, GLA_H, hk, hv), lambda bi, ci: (bi, 0, 0, 0, 0))
    return pl.pallas_call(
        functools.partial(_gla_kernel, float(hk) ** -0.5),
        grid=(b, n),
        in_specs=specs(False) + specs(True) + [
            pl.BlockSpec((2, rank, dk), lambda bi, ci: (0, 0, 0)),
            pl.BlockSpec((2, 1, dk), lambda bi, ci: (0, 0, 0)),
            sspec],
        out_specs=[ospec(False), ospec(True), sspec],
        out_shape=[jax.ShapeDtypeStruct((b * l, dv), F32), jax.ShapeDtypeStruct((b * l, dv), F32),
                   jax.ShapeDtypeStruct((b, 2, GLA_H, hk, hv), F32)],
        scratch_shapes=[pltpu.VMEM((GLA_H, hk, hv), F32), pltpu.VMEM((GLA_H, hk, hv), F32)],
        compiler_params=_cparams("parallel", "arbitrary"),
        name="gla_scan",
    )(p, p, p, gkd, p, p, p, gkd, gk2, gk_b.reshape(2, 1, dk), s0)


def _gla_post_kernel(hv, of_ref, ob_ref, g_ref, nw_ref, o_ref):
    o = of_ref[...] + ob_ref[...]
    g = g_ref[...]
    for h in range(GLA_H):
        oh = o[:, h * hv:(h + 1) * hv]
        gh = g[:, h * hv:(h + 1) * hv]
        oh = oh * lax.rsqrt(jnp.mean(oh * oh, axis=-1, keepdims=True) + EPS) * nw_ref[...]
        o_ref[:, h * hv:(h + 1) * hv] = (oh * (gh * jax.nn.sigmoid(gh))).astype(o_ref.dtype)


def _gla_post(geo, o_f, o_b, p, dk, dv, norm_w):
    nt = o_f.shape[0]
    tm = _pow2_tile(geo.tm, 256)
    hv = dv // GLA_H
    tspec = pl.BlockSpec((tm, dv), lambda i: (i, 0))
    return pl.pallas_call(
        functools.partial(_gla_post_kernel, hv),
        grid=(nt // tm,),
        in_specs=[tspec, tspec, pl.BlockSpec((tm, dv), lambda i: (i, 2 * dk // dv + 1)),
                  pl.BlockSpec((1, hv), lambda i: (0, 0))],
        out_specs=tspec,
        out_shape=jax.ShapeDtypeStruct((nt, dv), BF16),
        compiler_params=_cparams("parallel"),
        name="gla_post",
    )(o_f, o_b, p, norm_w.reshape(1, hv))


def _gla_mixer(geo, p, gkd, s0_sample, gk2, gk_b, norm_w, dk, dv):
    hk, hv = dk // GLA_H, dv // GLA_H
    s0_p = jnp.zeros((geo.bp, 2, GLA_H, hk, hv), F32)
    of_p, ob_p, s_p = _gla_scan(p, gkd, 0, geo.bp, geo.lp, dk, dv, gk2, gk_b, s0_p)
    of_s, ob_s, _ = _gla_scan(p, gkd, geo.np_, geo.bs, geo.ls, dk, dv, gk2, gk_b, s0_sample)
    o_f = jnp.concatenate([of_p, of_s], axis=0)
    o_b = jnp.concatenate([ob_p, ob_s], axis=0)
    return _gla_post(geo, o_f, o_b, p, dk, dv, norm_w), s_p


def _conv_gate_kernel(geo, tm, g_ref, gp_ref, gn_ref, v_ref, w_ref, o_ref):
    i = pl.program_id(0)
    is_prompt = i * tm < geo.np_
    halo = GRID_W
    n_ext = tm + 2 * halo

    def conv(lseq, ncol):
        ext = jnp.concatenate([gp_ref[...], g_ref[...], gn_ref[...]], axis=0)
        epos = (lax.broadcasted_iota(jnp.int32, (n_ext, 1), 0) + (i * tm - halo)) & (lseq - 1)
        scol = epos & (ncol - 1)
        taps = {-1: pltpu.roll(jnp.where(scol == ncol - 1, 0.0, ext), 1, axis=0),
                0: ext,
                1: pltpu.roll(jnp.where(scol == 0, 0.0, ext), n_ext - 1, axis=0)}

        def row_sum(di):
            start = halo + di * GRID_W
            return sum(taps[dj][start:start + tm, :] * w_ref[di + 1, dj + 1] for dj in (-1, 0, 1))

        acc = row_sum(0)
        nrow = lseq // ncol
        if nrow > 1:
            assert ncol == GRID_W
            tpos = (lax.broadcasted_iota(jnp.int32, (tm, 1), 0) + i * tm) & (lseq - 1)
            row = tpos >> int(math.log2(ncol))
            acc = acc + jnp.where(row >= 1, row_sum(-1), 0.0) + jnp.where(row <= nrow - 2, row_sum(1), 0.0)
        o_ref[...] = (acc * jax.nn.sigmoid(acc) * v_ref[...]).astype(o_ref.dtype)

    @pl.when(is_prompt)
    def _():
        conv(geo.lp, geo.lp)

    @pl.when(jnp.logical_not(is_prompt))
    def _():
        conv(geo.ls, GRID_W)


def _conv_gate(geo, u, conv_w):
    nt = u.shape[0]
    f = u.shape[1] // 2
    tm = _pow2_tile(geo.tm, 256)
    assert tm % GRID_W == 0
    tc = _col_tile(f, 512)
    ncb = f // tc
    r = tm // GRID_W
    last = nt // GRID_W - 1
    return pl.pallas_call(
        functools.partial(_conv_gate_kernel, geo, tm),
        grid=(nt // tm, ncb),
        in_specs=[pl.BlockSpec((tm, tc), lambda i, j: (i, j)),
                  pl.BlockSpec((GRID_W, tc), lambda i, j: (jnp.maximum(i * r - 1, 0), j)),
                  pl.BlockSpec((GRID_W, tc), lambda i, j: (jnp.minimum((i + 1) * r, last), j)),
                  pl.BlockSpec((tm, tc), lambda i, j: (i, ncb + j)),
                  pl.BlockSpec((3, 3, 1, tc), lambda i, j: (0, 0, 0, j))],
        out_specs=pl.BlockSpec((tm, tc), lambda i, j: (i, j)),
        out_shape=jax.ShapeDtypeStruct((nt, f), BF16),
        compiler_params=_cparams("parallel", "arbitrary"),
        name="conv_gate",
    )(u, u, u, u, conv_w.reshape(3, 3, 1, f))


def _pad_cols(w, mult):
    n = w.shape[-1]
    pad = _round_up(n, mult) - n
    return jnp.pad(w, [(0, 0)] * (w.ndim - 1) + [(0, pad)]) if pad else w


def kernel(x_prompt, x_sample, state_rwkv, state_gla, c, c_ctx, w_ada, b_ada, norm_mix, norm_ffn, ffn_w_up, ffn_conv, ffn_w_down, norm_final, ev_w_in, ev_w_out, hy_short_w, hy_short_b, hy_w1, hy_b1, hy_w2, hy_b2, hy_w3, hy_freq, hy_bias, rw_mu, rw_w0, rw_w2, rw_a0, rw_a2, rw_g2, rw_kk, rw_ka, rw_rk, rw_ln_w, rw_ln_b, od_w_in, od_w_out, gla_gk2, gla_gk_b, gla_norm):
    bp, lp, d = x_prompt.shape
    bs, ls, _ = x_sample.shape
    geo = _Geo(bp, lp, bs, ls, d)
    depth = w_ada.shape[0]
    x = jnp.concatenate([x_prompt.reshape(geo.np_, d), x_sample.reshape(geo.ns, d)], axis=0)

    cond = jnp.concatenate([c_ctx[None, :], c], axis=0)
    cond = jnp.pad(cond, ((0, SUBLANES - cond.shape[0]), (0, 0)))
    mods_all = _adaln(cond, w_ada, b_ada)

    rw_states, gla_states = [], []
    for layer in range(depth):
        mods = mods_all[layer].reshape(SUBLANES * 6, 1, d)
        if layer % 2 == 0:
            e = layer // 2
            dh = hy_bias.shape[1]
            dr = rw_kk.shape[1]
            perm = _head_minor_perm(dr)
            perm3 = np.concatenate([perm, dr + perm, 2 * dr + perm, np.arange(3 * dr, ev_w_in.shape[2] - 3 * dh)])
            w_hy = ev_w_in[e][:, :3 * dh].astype(BF16)
            w_rw = _pad_cols(ev_w_in[e][:, 3 * dh:][:, perm3], 512).astype(BF16)
            p_hy = _norm_mm(geo, x, norm_mix[layer], mods, 0, w_hy)
            p_rw = _norm_mm(geo, x, norm_mix[layer], mods, 0, w_rw)
            x0, u = _hy_prep(geo, p_hy, hy_short_w[e], hy_short_b[e])
            filt_p = _hyena_filter(lp, hy_w1[e], hy_b1[e], hy_w2[e], hy_b2[e], hy_w3[e], hy_freq[e], dh)
            filt_s = _hyena_filter(ls, hy_w1[e], hy_b1[e], hy_w2[e], hy_b2[e], hy_w3[e], hy_freq[e], dh)
            y_hy = _hyena_long_conv(geo, x0, u, filt_p, filt_s, hy_bias[e])
            mu = _pad_cols(rw_mu[e][perm3], 512)
            y_rw, s_ctx = _rwkv_mixer(geo, p_rw, state_rwkv[:, e], mu, rw_w0[e][:, perm], rw_w2[e][:, :, perm],
                                      rw_a0[e][:, perm], rw_a2[e][:, :, perm], rw_g2[e][:, perm], rw_kk[e][perm],
                                      rw_ka[e][perm], rw_rk[e].reshape(dr)[perm], rw_ln_w[e][perm], rw_ln_b[e][perm])
            rw_states.append(s_ctx)
            mixed = jnp.concatenate([y_hy, y_rw], axis=-1)
            w_out = jnp.concatenate([ev_w_out[e][:dh], ev_w_out[e][dh:][perm]], axis=0).astype(BF16)
            x = _res_mm(geo, mixed, w_out, x, mods, 2)
        else:
            o = layer // 2
            dk = gla_gk2.shape[3]
            dv = gla_norm.shape[1] * GLA_H
            w_main = od_w_in[o][:, :2 * dk + 2 * dv].astype(BF16)
            w_gk = _pad_cols(od_w_in[o][:, 2 * dk + 2 * dv:], LANES).astype(BF16)
            p = _norm_mm(geo, x, norm_mix[layer], mods, 0, w_main)
            gkd = _norm_mm(geo, x, norm_mix[layer], mods, 0, w_gk)
            y, s_ctx = _gla_mixer(geo, p, gkd, state_gla[:, o], gla_gk2[o], gla_gk_b[o], gla_norm[o], dk, dv)
            gla_states.append(s_ctx)
            x = _res_mm(geo, y, od_w_out[o].astype(BF16), x, mods, 2)
        u_ffn = _norm_mm(geo, x, norm_ffn[layer], mods, 3, ffn_w_up[layer].astype(BF16))
        act = _conv_gate(geo, u_ffn, ffn_conv[layer])
        x = _res_mm(geo, act, ffn_w_down[layer].astype(BF16), x, mods, 5)

    y = _final_norm(geo, x, norm_final)
    y_prompt = y[:geo.np_].reshape(bp, lp, d)
    y_sample = y[geo.np_:].reshape(bs, ls, d)
    return (y_prompt, y_sample, jnp.stack(rw_states, axis=1), jnp.stack(gla_states, axis=1))
```

```python
import functools
import math

import numpy as np
import jax
import jax.numpy as jnp
from jax import lax
from jax.experimental import pallas as pl
from jax.experimental.pallas import tpu as pltpu

F32 = jnp.float32
BF16 = jnp.bfloat16
HI = lax.Precision.HIGHEST

EPS = 1e-6
RW_HEAD = 64
RW_GN_EPS = 64e-5
GLA_H = 4
GLA_GATE_NORM = 16.0
GLA_CHUNK = 64
GRID_W = 64
HY_TARGET = 1e-2
HY_FAST = 0.3
HY_SLOW = 1.5

LANES = 128
SUBLANES = 8
MXU_DIM = 256
VMEM_LIMIT = 56 * 1024 * 1024
FFT_MINOR = 128


def _cparams(*sem):
    return pltpu.CompilerParams(dimension_semantics=sem, vmem_limit_bytes=VMEM_LIMIT)


def _round_up(n, m):
    return (n + m - 1) // m * m


def _pow2_tile(n, pref):
    t = 1
    while t * 2 <= pref and n % (t * 2) == 0:
        t *= 2
    return t


def _col_tile(n, pref):
    if n % LANES:
        return n
    divisors = [t for t in range(LANES, min(n, pref) + 1, LANES) if n % t == 0]
    mxu_wide = [t for t in divisors if t % MXU_DIM == 0]
    return max(mxu_wide or divisors)


class _Geo:
    def __init__(self, bp, lp, bs, ls, d):
        self.bp, self.lp, self.bs, self.ls, self.d = bp, lp, bs, ls, d
        self.np_, self.ns = bp * lp, bs * ls
        self.nt = self.np_ + self.ns
        self.tm = _pow2_tile(math.gcd(self.np_, ls), 512)
        assert lp & (lp - 1) == 0 and ls & (ls - 1) == 0, "sequence lengths must be powers of two"
        assert self.np_ % ls == 0, "sample rows must start on a sequence-length boundary"
        assert bs + 1 <= SUBLANES

    def row_tile(self, pref):
        return _pow2_tile(math.gcd(self.np_, self.ls), pref)

    def modrow(self, i, tm):
        r0 = i * tm
        return jnp.where(r0 < self.np_, 0, 1 + (r0 - self.np_) // self.ls)

    def seq_len(self, i, tm):
        return jnp.where(i * tm < self.np_, self.lp, self.ls)

    def mirror_tile(self, i, tm):
        def mirrored(tiles_per_seq):
            return (i // tiles_per_seq) * tiles_per_seq + (tiles_per_seq - 1 - i % tiles_per_seq)
        return jnp.where(i * tm < self.np_, mirrored(self.lp // tm), mirrored(self.ls // tm))

    @staticmethod
    def streams_per_group(dr):
        return LANES // (dr // RW_HEAD)

    def stream_slot(self, i, tm, direction, dr):
        r0 = i * tm
        batch = jnp.where(r0 < self.np_, r0 // self.lp, (r0 - self.np_) // self.ls)
        nbatch = jnp.where(r0 < self.np_, self.bp, self.bs)
        return (direction * nbatch + batch) % self.streams_per_group(dr)


def _adaln_kernel(c_ref, w_ref, b_ref, o_ref):
    c = c_ref[...]
    s = c * jax.nn.sigmoid(c)
    o_ref[0] = jnp.dot(s, w_ref[0], precision=HI, preferred_element_type=F32) + b_ref[0]


def _adaln(cond, w_ada, b_ada):
    depth, d, n = w_ada.shape
    tn = _col_tile(n, 1024)
    return pl.pallas_call(
        _adaln_kernel,
        grid=(depth, n // tn),
        in_specs=[pl.BlockSpec((SUBLANES, d), lambda l, j: (0, 0)),
                  pl.BlockSpec((1, d, tn), lambda l, j: (l, 0, j)),
                  pl.BlockSpec((1, 1, tn), lambda l, j: (l, 0, j))],
        out_specs=pl.BlockSpec((1, SUBLANES, tn), lambda l, j: (l, 0, j)),
        out_shape=jax.ShapeDtypeStruct((depth, SUBLANES, n), F32),
        compiler_params=_cparams("parallel", "arbitrary"),
        name="adaln",
    )(cond, w_ada, b_ada.reshape(depth, 1, n))


def _norm_mm_kernel(x_ref, g_ref, sh_ref, sc_ref, w_ref, o_ref, xn_ref):
    @pl.when(pl.program_id(1) == 0)
    def _():
        x = x_ref[...]
        ms = jnp.mean(x * x, axis=-1, keepdims=True)
        y = x * lax.rsqrt(ms + EPS) * g_ref[...]
        xn_ref[...] = (y * (1.0 + sc_ref[0]) + sh_ref[0]).astype(BF16)

    o_ref[...] = jnp.dot(xn_ref[...], w_ref[...], preferred_element_type=F32).astype(o_ref.dtype)


def _norm_mm(geo, x, g, mods, which_shift, w, out_dtype=F32, tm_pref=1024, tn_pref=1024):
    nt, d = x.shape
    n = w.shape[1]
    tm = geo.row_tile(tm_pref)
    tn = _col_tile(n, tn_pref)

    def mod_map(which):
        return lambda i, j: (geo.modrow(i, tm) * 6 + which, 0, 0)

    return pl.pallas_call(
        _norm_mm_kernel,
        grid=(nt // tm, n // tn),
        in_specs=[pl.BlockSpec((tm, d), lambda i, j: (i, 0)),
                  pl.BlockSpec((1, d), lambda i, j: (0, 0)),
                  pl.BlockSpec((1, 1, d), mod_map(which_shift)),
                  pl.BlockSpec((1, 1, d), mod_map(which_shift + 1)),
                  pl.BlockSpec((d, tn), lambda i, j: (0, j))],
        out_specs=pl.BlockSpec((tm, tn), lambda i, j: (i, j)),
        out_shape=jax.ShapeDtypeStruct((nt, n), out_dtype),
        scratch_shapes=[pltpu.VMEM((tm, d), BF16)],
        compiler_params=_cparams("parallel", "arbitrary"),
        name="norm_mm",
    )(x, g.reshape(1, d), mods, mods, w)


def _res_mm_kernel(a_ref, w_ref, res_ref, gt_ref, o_ref):
    acc = jnp.dot(a_ref[...], w_ref[...], preferred_element_type=F32)
    o_ref[...] = res_ref[...] + gt_ref[0] * acc


def _res_mm2_kernel(a1_ref, a2_ref, w_ref, res_ref, gt_ref, o_ref):
    k1 = a1_ref.shape[1]
    acc = jnp.dot(a1_ref[...], w_ref[0:k1, :], preferred_element_type=F32)
    acc = acc + jnp.dot(a2_ref[...], w_ref[k1:, :], preferred_element_type=F32)
    o_ref[...] = res_ref[...] + gt_ref[0] * acc


def _res_mm(geo, a, w, res, mods, which_gate, tm_pref=1024):
    parts = a if isinstance(a, (tuple, list)) else (a,)
    nt = parts[0].shape[0]
    k = sum(p.shape[1] for p in parts)
    n = w.shape[1]
    tm = geo.row_tile(tm_pref)
    tn = _col_tile(n, 1024 if k <= 2048 else 512)
    return pl.pallas_call(
        _res_mm_kernel if len(parts) == 1 else _res_mm2_kernel,
        grid=(nt // tm, n // tn),
        in_specs=[pl.BlockSpec((tm, p.shape[1]), lambda i, j: (i, 0)) for p in parts] + [
                  pl.BlockSpec((k, tn), lambda i, j: (0, j)),
                  pl.BlockSpec((tm, tn), lambda i, j: (i, j)),
                  pl.BlockSpec((1, 1, tn), lambda i, j: (geo.modrow(i, tm) * 6 + which_gate, 0, j))],
        out_specs=pl.BlockSpec((tm, tn), lambda i, j: (i, j)),
        out_shape=jax.ShapeDtypeStruct((nt, n), F32),
        compiler_params=_cparams("parallel", "arbitrary"),
        name="res_mm",
    )(*parts, w, res, mods)


def _final_norm_kernel(x_ref, g_ref, o_ref):
    x = x_ref[...]
    ms = jnp.mean(x * x, axis=-1, keepdims=True)
    o_ref[...] = x * lax.rsqrt(ms + EPS) * g_ref[...]


def _final_norm(geo, x, g):
    nt, d = x.shape
    tm = geo.tm
    return pl.pallas_call(
        _final_norm_kernel,
        grid=(nt // tm,),
        in_specs=[pl.BlockSpec((tm, d), lambda i: (i, 0)), pl.BlockSpec((1, d), lambda i: (0, 0))],
        out_specs=pl.BlockSpec((tm, d), lambda i: (i, 0)),
        out_shape=jax.ShapeDtypeStruct((nt, d), F32),
        compiler_params=_cparams("parallel"),
        name="final_norm",
    )(x, g.reshape(1, d))


def _seq_pos(geo, i, tm):
    lseq = geo.seq_len(i, tm)
    rid = lax.broadcasted_iota(jnp.int32, (tm, 1), 0) + i * tm
    return rid & (lseq - 1), lseq


def _neighbours(x, prev_row, next_row, tpos, lseq):
    tm = x.shape[0]
    rid = lax.broadcasted_iota(jnp.int32, (tm, 1), 0)
    xm = pltpu.roll(x, 1, axis=0)
    xm = jnp.where(rid == 0, prev_row, xm)
    xm = jnp.where(tpos == 0, 0.0, xm)
    xp = pltpu.roll(x, tm - 1, axis=0)
    xp = jnp.where(rid == tm - 1, next_row, xp)
    xp = jnp.where(tpos == lseq - 1, 0.0, xp)
    return xm, xp


def _halo_specs(nt, tm, width, col):
    r = tm // SUBLANES
    last = nt // SUBLANES - 1
    prev = pl.BlockSpec((SUBLANES, width), lambda i, j: (jnp.maximum(i * r - 1, 0), col(j)))
    nxt = pl.BlockSpec((SUBLANES, width), lambda i, j: (jnp.minimum((i + 1) * r, last), col(j)))
    return prev, nxt


def _hy_prep_kernel(geo, tm, *refs):
    (x0_ref, x0p_ref, x0n_ref, x1_ref, x1p_ref, x1n_ref, x2_ref, x2p_ref, x2n_ref,
     w_ref, b_ref, o0_ref, u_ref) = refs
    i = pl.program_id(0)
    tpos, lseq = _seq_pos(geo, i, tm)

    def conv(x_ref, p_ref, n_ref, g):
        x = x_ref[...]
        xm, xp = _neighbours(x, p_ref[SUBLANES - 1:SUBLANES, :], n_ref[0:1, :], tpos, lseq)
        return xm * w_ref[0, g] + x * w_ref[1, g] + xp * w_ref[2, g] + b_ref[g]

    c0 = conv(x0_ref, x0p_ref, x0n_ref, 0)
    c1 = conv(x1_ref, x1p_ref, x1n_ref, 1)
    c2 = conv(x2_ref, x2p_ref, x2n_ref, 2)
    o0_ref[...] = c0
    u_ref[...] = c1 * c2


def _hy_prep(geo, p_hy, short_w, short_b):
    nt = p_hy.shape[0]
    dh = p_hy.shape[1] // 3
    tm = _pow2_tile(geo.tm, 256)
    tc = _col_tile(dh, 512)
    ncb = dh // tc
    specs = []
    for g in range(3):
        col = (lambda j, g=g: g * ncb + j)
        specs.append(pl.BlockSpec((tm, tc), lambda i, j, col=col: (i, col(j))))
        specs.extend(_halo_specs(nt, tm, tc, col))
    specs.append(pl.BlockSpec((3, 3, 1, tc), lambda i, j: (0, 0, 0, j)))
    specs.append(pl.BlockSpec((3, 1, tc), lambda i, j: (0, 0, j)))
    out_spec = pl.BlockSpec((tm, tc), lambda i, j: (i, j))
    w4 = short_w.reshape(3, 3, 1, dh)
    b3 = short_b.reshape(3, 1, dh)
    return pl.pallas_call(
        functools.partial(_hy_prep_kernel, geo, tm),
        grid=(nt // tm, ncb),
        in_specs=specs,
        out_specs=[out_spec, out_spec],
        out_shape=[jax.ShapeDtypeStruct((nt, dh), F32)] * 2,
        compiler_params=_cparams("parallel", "arbitrary"),
        name="hy_prep",
    )(*([p_hy] * 9), w4, b3)


def _lmm_kernel(has_epi, *refs):
    if has_epi:
        a_ref, x_ref, x0_ref, u_ref, bias_ref, o_ref = refs
    else:
        a_ref, x_ref, o_ref = refs
    acc = jnp.dot(a_ref[0], x_ref[0], precision=HI, preferred_element_type=F32)
    if has_epi:
        acc = x0_ref[0] * (acc + u_ref[0] * bias_ref[...])
    o_ref[0] = acc.astype(o_ref.dtype)


def _lmm(a, x, epi=None, out_dtype=F32):
    gm, m, k = a.shape
    g, _, n = x.shape
    tn = _col_tile(n, 2048)
    xspec = pl.BlockSpec((1, k, tn), lambda gi, j: (gi, 0, j))
    ospec = pl.BlockSpec((1, m, tn), lambda gi, j: (gi, 0, j))
    specs = [pl.BlockSpec((1, m, k), lambda gi, j: (gi % gm, 0, 0)), xspec]
    args = [a, x]
    if epi is not None:
        x0, u, bias = epi
        specs += [ospec, ospec, pl.BlockSpec((1, tn), lambda gi, j: (0, j))]
        args += [x0, u, bias]
    return pl.pallas_call(
        functools.partial(_lmm_kernel, epi is not None),
        grid=(g, n // tn),
        in_specs=specs,
        out_specs=ospec,
        out_shape=jax.ShapeDtypeStruct((g, m, n), out_dtype),
        compiler_params=_cparams("parallel", "arbitrary"),
        name="dft_lmm",
    )(*args)


def _spec_kernel(has_epi, *refs):
    if has_epi:
        mf_ref, h_ref, mi_ref, x_ref, x0_ref, bias_ref, o_ref = refs
    else:
        mf_ref, h_ref, mi_ref, x_ref, o_ref = refs
    x = x_ref[0]
    f = jnp.dot(mf_ref[0], x, precision=HI, preferred_element_type=F32)
    r = f.shape[0] // 2
    fr, fi = f[:r], f[r:]
    hr, hi = h_ref[0, :r], h_ref[0, r:]
    y = jnp.concatenate([fr * hr - fi * hi, fr * hi + fi * hr], axis=0)
    out = jnp.dot(mi_ref[0], y, precision=HI, preferred_element_type=F32)
    if has_epi:
        out = x0_ref[0] * (out + x * bias_ref[...])
    o_ref[0] = out.astype(o_ref.dtype)


def _spectral(mf, h, mi, x, epi=None, out_dtype=F32):
    gm, r2, k = mf.shape
    kout = mi.shape[1]
    g, _, c = x.shape
    tc = _col_tile(c, 512)
    xspec = pl.BlockSpec((1, k, tc), lambda gi, j: (gi, 0, j))
    ospec = pl.BlockSpec((1, kout, tc), lambda gi, j: (gi, 0, j))
    specs = [pl.BlockSpec((1, r2, k), lambda gi, j: (gi % gm, 0, 0)),
             pl.BlockSpec((1, r2, tc), lambda gi, j: (gi % gm, 0, j)),
             pl.BlockSpec((1, kout, r2), lambda gi, j: (gi % gm, 0, 0)),
             xspec]
    args = [mf, h, mi, x]
    if epi is not None:
        x0, bias = epi
        specs += [ospec, pl.BlockSpec((1, tc), lambda gi, j: (0, j))]
        args += [x0, bias]
    return pl.pallas_call(
        functools.partial(_spec_kernel, epi is not None),
        grid=(g, c // tc),
        in_specs=specs,
        out_specs=ospec,
        out_shape=jax.ShapeDtypeStruct((g, kout, c), out_dtype),
        compiler_params=_cparams("parallel", "arbitrary"),
        name="dft_spectral",
    )(*args)


def _dft_tables_direct(l):
    n = 2 * l
    nf = l + 1
    r = _round_up(nf, SUBLANES)
    kk = np.arange(nf)[:, None].astype(np.float64)
    ang_half = 2.0 * np.pi * ((kk * np.arange(l)[None, :]) % n) / n
    ang_full = 2.0 * np.pi * ((kk * np.arange(n)[None, :]) % n) / n

    def fwd(ang):
        m = np.zeros((2 * r, ang.shape[1]))
        m[:nf] = np.cos(ang)
        m[r:r + nf] = -np.sin(ang)
        return m

    coef = np.full((nf,), 2.0 / n)
    coef[0] = coef[-1] = 1.0 / n
    inv = np.zeros((l, 2 * r))
    inv[:, :nf] = (np.cos(ang_half) * coef[:, None]).T
    inv[:, r:r + nf] = (-np.sin(ang_half) * coef[:, None]).T
    f32 = lambda a: jnp.asarray(a[None], F32)
    return f32(fwd(ang_half)), f32(fwd(ang_full)), f32(inv)


def _dft_tables_two_level(l):
    n = 2 * l
    n1 = FFT_MINOR
    n2 = n // n1
    nb = n2 // 2 + 1
    nbp = _round_up(nb, 4)
    k2 = np.arange(nb)[:, None].astype(np.float64)
    ang2 = 2.0 * np.pi * ((k2 * np.arange(n2)[None, :]) % n2) / n2
    ms_full = np.zeros((2 * nbp, n2))
    ms_full[0:2 * nb:2] = np.cos(ang2)
    ms_full[1:2 * nb:2] = -np.sin(ang2)
    ms_half = ms_full[:, :n2 // 2]
    coef = np.full((nb,), 2.0 / n)
    coef[0] = coef[-1] = 1.0 / n
    ms_inv = np.zeros((n2 // 2, 2 * nbp))
    ms_inv[:, 0:2 * nb:2] = (np.cos(ang2[:, :n2 // 2]) * coef[:, None]).T
    ms_inv[:, 1:2 * nb:2] = (-np.sin(ang2[:, :n2 // 2]) * coef[:, None]).T

    a_fwd = np.zeros((nbp, 2 * n1, 2 * n1))
    a_inv = np.zeros((nbp, 2 * n1, 2 * n1))
    i1 = np.arange(n1).astype(np.float64)
    for b in range(nb):
        ph = 2.0 * np.pi * (((np.outer(i1, i1) * n2) + (i1[None, :] * b)) % n) / n
        wr, wi = np.cos(ph), -np.sin(ph)
        a_fwd[b] = np.block([[wr, -wi], [wi, wr]])
        ph = 2.0 * np.pi * (((np.outer(i1, i1) * n2) + (i1[:, None] * b)) % n) / n
        pr, pi_ = np.cos(ph), np.sin(ph)
        a_inv[b] = np.block([[pr, -pi_], [pi_, pr]])
    f32 = lambda a: jnp.asarray(a, F32)
    return (f32(ms_half[None]), f32(ms_full[None]), f32(ms_inv[None]), f32(a_fwd), f32(a_inv), n1, n2, nbp)


def _hyena_filter(l, w1, b1, w2, b2, w3, freq, dh):
    pos = jnp.arange(l, dtype=F32)[:, None]
    t = pos / max(l - 1, 1)
    emb = w1.shape[0]
    bands = (emb - 1) // 2
    fb = jnp.linspace(1e-4, bands - 1, bands, dtype=F32)
    ang = (2.0 * math.pi / l) * pos * fb
    z = jnp.concatenate([t, jnp.cos(ang), -jnp.sin(ang)], axis=-1)
    h = jnp.sin(freq[0] * (jnp.dot(z, w1, precision=HI) + b1))
    h = jnp.sin(freq[1] * (jnp.dot(h, w2, precision=HI) + b2))
    h = jnp.dot(h, w3, precision=HI).reshape(l, 2, dh)
    deltas = jnp.abs(jnp.linspace(math.log(HY_TARGET) / HY_SLOW, math.log(HY_TARGET) / HY_FAST, dh, dtype=F32))
    h = h * jnp.exp(-t[:, :, None] * deltas)
    filt = jnp.concatenate([h[:, 0], jnp.zeros((1, dh), F32), jnp.flip(h[1:, 1], axis=0)], axis=0)
    return filt / jnp.sum(jnp.abs(filt), axis=0, keepdims=True)


def _hyena_long_conv(geo, x0, u, filt_p, filt_s, bias):
    dh = u.shape[1]
    bias2 = bias.reshape(1, dh)
    mf_half, mf_full, mi = _dft_tables_direct(geo.lp)
    h_p = _lmm(mf_full, filt_p[None])
    up = u[:geo.np_].reshape(geo.bp, geo.lp, dh)
    x0p = x0[:geo.np_].reshape(geo.bp, geo.lp, dh)
    y_p = _spectral(mf_half, h_p, mi, up, epi=(x0p, bias2), out_dtype=BF16)
    ms_half, ms_full, ms_inv, a_fwd, a_inv, n1, n2, nbp = _dft_tables_two_level(geo.ls)
    hs = _lmm(ms_full, filt_s.reshape(1, n2, n1 * dh))
    hs = _lmm(a_fwd, hs.reshape(nbp, 2 * n1, dh))
    us = u[geo.np_:].reshape(geo.bs, n2 // 2, n1 * dh)
    x0s = x0[geo.np_:].reshape(geo.bs, n2 // 2, n1 * dh)
    xs = _lmm(ms_half, us)
    zs = _spectral(a_fwd, hs, a_inv, xs.reshape(geo.bs * nbp, 2 * n1, dh))
    zs = zs.reshape(geo.bs, 2 * nbp, n1 * dh)
    y_s = _lmm(ms_inv, zs, epi=(x0s, us, jnp.tile(bias2, (1, n1))), out_dtype=BF16)
    return jnp.concatenate([y_p.reshape(geo.np_, dh), y_s.reshape(geo.ns, dh)], axis=0)


def _head_sum(x):
    dr = x.shape[1]
    nheads = dr // RW_HEAD
    assert LANES % nheads == 0
    t = x[:, 0:LANES]
    for q in range(1, dr // LANES):
        t = t + x[:, q * LANES:(q + 1) * LANES]
    shift = LANES // 2
    while shift >= nheads:
        t = t + pltpu.roll(t, shift, axis=1)
        shift //= 2
    return jnp.concatenate([t] * (dr // LANES), axis=1)


def _flip_rows(x):
    tm, c = x.shape
    x3 = x.reshape(tm // SUBLANES, SUBLANES, c)
    rid = lax.broadcasted_iota(jnp.int32, (1, SUBLANES, 1), 1)
    for s in (4, 2, 1):
        x3 = jnp.where((rid & s) != 0, pltpu.roll(x3, s, axis=1), pltpu.roll(x3, SUBLANES - s, axis=1))
    return jnp.concatenate([x3[j] for j in reversed(range(tm // SUBLANES))], axis=0)


def _rw_prep_kernel(geo, tm, dr, *refs):
    (x_ref, xp_ref, xn_ref, mu_ref, w0_ref, w2_ref, a0_ref, a2_ref, g2_ref, kk_ref, ka_ref, rk_ref,
     rf_o, nkkf_o, vf_o, wf_o, kxf_o, bf_o, rb_o, nkkb_o, vb_o, wb_o, kxb_o, bb_o, g_o, bon_o) = refs
    i = pl.program_id(0)
    tpos, lseq = _seq_pos(geo, i, tm)
    x = x_ref[...]
    xm, xp = _neighbours(x, xp_ref[SUBLANES - 1:SUBLANES, :], xn_ref[0:1, :], tpos, lseq)
    x = x + mu_ref[...] * (0.5 * (xm + xp) - x)
    r = x[:, 0:dr]
    k = x[:, dr:2 * dr]
    v = x[:, 2 * dr:3 * dr]
    wd = x[:, 3 * dr:3 * dr + LANES]
    ad = x[:, 3 * dr + LANES:3 * dr + 2 * LANES]
    gd = x[:, 3 * dr + 2 * LANES:3 * dr + 3 * LANES]
    wl = w0_ref[...] + jnp.dot(jnp.tanh(wd), w2_ref[...], precision=HI, preferred_element_type=F32)
    w_log = -jax.nn.softplus(-wl) - 0.5
    decay = jnp.exp(-jnp.exp(w_log))
    a = jax.nn.sigmoid(a0_ref[...] + jnp.dot(ad, a2_ref[...], precision=HI, preferred_element_type=F32))
    g_o[...] = jnp.dot(jax.nn.sigmoid(gd), g2_ref[...], precision=HI, preferred_element_type=F32)
    kk = k * kk_ref[...]
    kk = kk * lax.rsqrt(_head_sum(kk * kk) + 1e-12)
    a_0, a_1 = a[:, :dr], a[:, dr:]
    kx0 = k * (1.0 + (a_0 - 1.0) * ka_ref[...])
    kx1 = k * (1.0 + (a_1 - 1.0) * ka_ref[...])
    bon_o[...] = _head_sum(r * (kx0 + kx1) * rk_ref[...]) * v

    slab = LANES // geo.streams_per_group(dr)

    def emit(o_ref, val, direction):
        if direction:
            val = _flip_rows(val)
        shift = geo.stream_slot(i, tm, direction, dr) * slab
        for q in range(dr // LANES):
            o_ref[:, q * LANES:(q + 1) * LANES] = pltpu.roll(val[:, q * LANES:(q + 1) * LANES], shift, axis=1)

    nkk = -kk
    for o_f, o_b, val_f, val_b in ((rf_o, rb_o, r, r), (nkkf_o, nkkb_o, nkk, nkk), (vf_o, vb_o, v, v),
                                   (wf_o, wb_o, decay[:, :dr], decay[:, dr:]), (kxf_o, kxb_o, kx0, kx1),
                                   (bf_o, bb_o, kk * a_0, kk * a_1)):
        emit(o_f, val_f, 0)
        emit(o_b, val_b, 1)


def _blockdiag2(w):
    _, r, c = w.shape
    z = jnp.zeros((r, c), w.dtype)
    return jnp.concatenate([jnp.concatenate([w[0], z], axis=1), jnp.concatenate([z, w[1]], axis=1)], axis=0)


def _head_minor_perm(dr):
    return np.arange(dr).reshape(dr // RW_HEAD, RW_HEAD).T.reshape(-1)


def _rw_prep(geo, p_rw, mu, w0, w2, a0, a2, g2, k_k, k_a, r_k):
    nt, width = p_rw.shape
    dr = k_k.shape[0]
    assert 2 * w2.shape[1] == LANES and 2 * a2.shape[1] == LANES and g2.shape[0] == LANES
    tm = _pow2_tile(min(geo.tm, geo.lp), 128)
    full = lambda shape: pl.BlockSpec(shape, lambda i, j: tuple(0 for _ in shape))
    prev, nxt = _halo_specs(nt, tm, width, lambda j: 0)
    consts = [mu.reshape(1, width), w0.reshape(1, 2 * dr), _blockdiag2(w2), a0.reshape(1, 2 * dr),
              _blockdiag2(a2), g2, k_k.reshape(1, dr), k_a.reshape(1, dr), r_k.reshape(1, dr)]
    ospec = pl.BlockSpec((tm, dr), lambda i, j: (i, 0))
    mspec = pl.BlockSpec((tm, dr), lambda i, j: (geo.mirror_tile(i, tm), 0))
    return pl.pallas_call(
        functools.partial(_rw_prep_kernel, geo, tm, dr),
        grid=(nt // tm, 1),
        in_specs=[pl.BlockSpec((tm, width), lambda i, j: (i, 0)), prev, nxt] + [full(c.shape) for c in consts],
        out_specs=[ospec] * 6 + [mspec] * 6 + [ospec] * 2,
        out_shape=[jax.ShapeDtypeStruct((nt, dr), F32)] * 14,
        compiler_params=_cparams("parallel", "arbitrary"),
        name="rw_prep",
    )(p_rw, p_rw, p_rw, *consts)


def _rw_scan_kernel(tc, r_ref, nkk_ref, v_ref, w_ref, kx_ref, b_ref, s0_ref, y_ref, so_ref, s_ref):
    t_chunk = pl.program_id(1)

    @pl.when(t_chunk == 0)
    def _():
        s_ref[...] = s0_ref[...]

    nacc = 2
    vhalf = RW_HEAD // 2

    def row(ref, t, k):
        return ref[t // SUBLANES, k, pl.ds(t % SUBLANES, 1), :]

    def state_dot_neg_kk(t):
        acc = [jnp.zeros((RW_HEAD, LANES), F32) for _ in range(nacc)]
        for k in range(RW_HEAD):
            acc[k % nacc] = acc[k % nacc] + s_ref[k] * row(nkk_ref, t, k)
        return acc[0] + acc[1]

    def step(t, sa):
        t_next = jnp.minimum(t + 1, tc - 1)
        sa_next = []
        for half in range(2):
            vs = slice(half * vhalf, (half + 1) * vhalf)
            sa_h = sa[vs]
            vv = v_ref[t, vs, :]
            acc_y = [jnp.zeros((vhalf, LANES), F32) for _ in range(nacc)]
            acc_s = [jnp.zeros((vhalf, LANES), F32) for _ in range(nacc)]
            for k in range(RW_HEAD):
                new = s_ref[k, vs, :] * row(w_ref, t, k) + sa_h * row(b_ref, t, k) + vv * row(kx_ref, t, k)
                s_ref[k, vs, :] = new
                acc_y[k % nacc] = acc_y[k % nacc] + new * row(r_ref, t, k)
                acc_s[k % nacc] = acc_s[k % nacc] + new * row(nkk_ref, t_next, k)
            y_ref[t, vs, :] = acc_y[0] + acc_y[1]
            sa_next.append(acc_s[0] + acc_s[1])
        return jnp.concatenate(sa_next, axis=0)

    lax.fori_loop(0, tc, step, state_dot_neg_kk(0))

    @pl.when(t_chunk == pl.num_programs(1) - 1)
    def _():
        so_ref[...] = s_ref[...]


def _rw_scan(r, nkk, v, w, kx, b, s0):
    t_len, n, lanes = v.shape
    tc = _pow2_tile(t_len, 32)
    xspec = pl.BlockSpec((tc, n, LANES), lambda g, t: (t, 0, g))
    rspec = pl.BlockSpec((tc // SUBLANES, n, SUBLANES, LANES), lambda g, t: (t, 0, 0, g))
    sspec = pl.BlockSpec((n, n, LANES), lambda g, t: (0, 0, g))
    return pl.pallas_call(
        functools.partial(_rw_scan_kernel, tc),
        grid=(lanes // LANES, t_len // tc),
        in_specs=[rspec, rspec, xspec, rspec, rspec, rspec, sspec],
        out_specs=[xspec, sspec],
        out_shape=[jax.ShapeDtypeStruct((t_len, n, lanes), F32), jax.ShapeDtypeStruct((n, n, lanes), F32)],
        scratch_shapes=[pltpu.VMEM((n, n, LANES), F32)],
        compiler_params=_cparams("parallel", "arbitrary"),
        name="rw_scan",
    )(r, nkk, v, w, kx, b, s0)


def _merge_slabs(pieces, slab, offset):
    n = len(pieces)
    lane_slab = lax.broadcasted_iota(jnp.int32, pieces[0].shape, 1) // slab
    acc = pieces[n - 1]
    for p in range(n - 1):
        acc = jnp.where(lane_slab == (offset + p) % n, pieces[p], acc)
    return acc


def _to_streams_kernel(nsrc, ntens, slab, value_index, *refs):
    srcs, outs = refs[:ntens * nsrc], refs[ntens * nsrc:]
    for ti in range(ntens):
        tiles = [srcs[ti * nsrc + s][...] for s in range(nsrc)]
        tt = tiles[0].shape[0]
        for j in range(SUBLANES):
            merged = _merge_slabs(tiles, slab, j)
            shift = ((nsrc - j) % nsrc) * slab
            merged = pltpu.roll(merged, shift, axis=1) if shift else merged
            if ti == value_index:
                outs[ti][:, j, :] = merged
            else:
                outs[ti][:, j] = merged.reshape(tt // SUBLANES, SUBLANES, LANES)


def _to_streams(fwd, bwd, row0, nbatch, l, direction):
    ntens = len(fwd)
    dr = fwd[0].shape[1]
    nsrc = _Geo.streams_per_group(dr)
    assert nsrc == SUBLANES, "the stream layout assumes 16 heads (eight channels per 128-lane tile)"
    slab = LANES // nsrc
    tt = _pow2_tile(l, 128)
    if direction is None:
        assert 2 * nbatch == nsrc
        ngroups = 1
        source = lambda g, s: (s // nbatch, s % nbatch)
    else:
        assert nbatch % nsrc == 0
        ngroups = nbatch // nsrc
        source = lambda g, s: (direction, g * nsrc + s)
    specs, args = [], []
    for ti in range(ntens):
        for s in range(nsrc):
            d = source(0, s)[0]
            specs.append(pl.BlockSpec(
                (tt, LANES), lambda g, i, q, s=s: ((row0 + source(g, s)[1] * l) // tt + i, q)))
            args.append((bwd if d else fwd)[ti])
    value_index = 2
    vspec = pl.BlockSpec((tt, SUBLANES, LANES), lambda g, i, q: (i, q, g))
    rspec = pl.BlockSpec((tt // SUBLANES, SUBLANES, SUBLANES, LANES), lambda g, i, q: (i, q, 0, g))
    vshape = jax.ShapeDtypeStruct((l, RW_HEAD, ngroups * LANES), F32)
    rshape = jax.ShapeDtypeStruct((l // SUBLANES, RW_HEAD, SUBLANES, ngroups * LANES), F32)
    return pl.pallas_call(
        functools.partial(_to_streams_kernel, nsrc, ntens, slab, value_index),
        grid=(ngroups, l // tt, dr // LANES),
        in_specs=specs,
        out_specs=[vspec if ti == value_index else rspec for ti in range(ntens)],
        out_shape=[vshape if ti == value_index else rshape for ti in range(ntens)],
        compiler_params=_cparams("parallel", "parallel", "arbitrary"),
        name="to_streams",
    )(*args)


def _from_streams_kernel(nsrc, slab, y_ref, o_ref):
    rolled = [y_ref[:, j, :] if j == 0 else pltpu.roll(y_ref[:, j, :], j * slab, axis=1) for j in range(SUBLANES)]
    for s in range(nsrc):
        merged = _merge_slabs(rolled, slab, s)
        shift = ((nsrc - s) % nsrc) * slab
        o_ref[s] = pltpu.roll(merged, shift, axis=1) if shift else merged


def _from_streams(y, dr):
    l, _, lanes = y.shape
    nsrc = _Geo.streams_per_group(dr)
    slab = LANES // nsrc
    tt = _pow2_tile(l, 128)
    ngroups = lanes // LANES
    return pl.pallas_call(
        functools.partial(_from_streams_kernel, nsrc, slab),
        grid=(ngroups, l // tt, dr // LANES),
        in_specs=[pl.BlockSpec((tt, SUBLANES, LANES), lambda g, i, q: (i, q, g))],
        out_specs=pl.BlockSpec((nsrc, tt, LANES), lambda g, i, q: (g, i, q)),
        out_shape=jax.ShapeDtypeStruct((ngroups * nsrc, l, dr), F32),
        compiler_params=_cparams("parallel", "parallel", "arbitrary"),
        name="from_streams",
    )(y)


def _rw_post_kernel(prompt_tiles, yfp_ref, ybp_ref, yfs_ref, ybs_ref, bon_ref, g_ref, lnw_ref, lnb_ref, o_ref):
    is_prompt = pl.program_id(0) < prompt_tiles
    yf = jnp.where(is_prompt, yfp_ref[...], yfs_ref[...])
    yb = jnp.where(is_prompt, ybp_ref[...], ybs_ref[...])
    y = yf + _flip_rows(yb)
    inv_n = 1.0 / RW_HEAD
    mean = _head_sum(y) * inv_n
    yc = y - mean
    var = _head_sum(yc * yc) * inv_n
    yn = yc * lax.rsqrt(var + RW_GN_EPS) * lnw_ref[...] + lnb_ref[...]
    o_ref[...] = ((yn + bon_ref[...]) * g_ref[...]).astype(o_ref.dtype)


def _rw_post(geo, yf_p, yb_p, y_s, bonus, g, ln_w, ln_b):
    nt, dr = bonus.shape
    tm = _pow2_tile(min(geo.tm, geo.lp), 256)
    ntp, nts = geo.np_ // tm, geo.ns // tm
    tspec = pl.BlockSpec((tm, dr), lambda i: (i, 0))
    specs = [pl.BlockSpec((tm, dr), lambda i: (jnp.minimum(i, ntp - 1), 0)),
             pl.BlockSpec((tm, dr), lambda i: (jnp.where(i < ntp, geo.mirror_tile(i, tm), 0), 0)),
             pl.BlockSpec((tm, dr), lambda i: (jnp.maximum(i - ntp, 0), 0)),
             pl.BlockSpec((tm, dr), lambda i: (nts + jnp.where(i < ntp, 0, geo.mirror_tile(i, tm) - ntp), 0))]
    full = lambda a: pl.BlockSpec(a.shape, lambda i: (0, 0))
    consts = [ln_w.reshape(1, dr), ln_b.reshape(1, dr)]
    return pl.pallas_call(
        functools.partial(_rw_post_kernel, ntp),
        grid=(nt // tm,),
        in_specs=specs + [tspec, tspec] + [full(c) for c in consts],
        out_specs=tspec,
        out_shape=jax.ShapeDtypeStruct((nt, dr), BF16),
        compiler_params=_cparams("parallel"),
        name="rw_post",
    )(yf_p, yb_p, y_s, y_s, bonus, g, *consts)


def _rwkv_mixer(geo, p_rw, s0_sample, mu, w0, w2, a0, a2, g2, k_k, k_a, r_k, ln_w, ln_b):
    dr = k_k.shape[0]
    h = dr // RW_HEAD
    outs = _rw_prep(geo, p_rw, mu, w0, w2, a0, a2, g2, k_k, k_a, r_k)
    fwd, bwd, (g, bonus) = outs[0:6], outs[6:12], outs[12:14]
    s0 = s0_sample.transpose(4, 3, 1, 0, 2).reshape(RW_HEAD, RW_HEAD, 2 * geo.bs * h)
    y_s, _ = _rw_scan(*_to_streams(fwd, bwd, geo.np_, geo.bs, geo.ls, None), s0)
    y_s = _from_streams(y_s, dr)
    y_p, s_p = [], []
    for direction in range(2):
        zero = jnp.zeros((RW_HEAD, RW_HEAD, geo.bp * h), F32)
        y_d, s_d = _rw_scan(*_to_streams(fwd, bwd, 0, geo.bp, geo.lp, direction), zero)
        y_p.append(_from_streams(y_d, dr))
        s_p.append(s_d.reshape(RW_HEAD, RW_HEAD, geo.bp, h))
    states = jnp.stack(s_p).transpose(3, 0, 4, 2, 1)
    y_rw = _rw_post(geo, y_p[0].reshape(geo.np_, dr), y_p[1].reshape(geo.np_, dr), y_s.reshape(2 * geo.ns, dr),
                    bonus, g, ln_w, ln_b)
    return y_rw, states


def _gla_dir(q, k, v, gkd, gk2, gkb, s_ref, o_ref, reverse):
    c, dk = q.shape
    hk, hv = dk // GLA_H, v.shape[1] // GLA_H
    z = jnp.dot(gkd, gk2, precision=HI, preferred_element_type=F32) + gkb
    lg = (jnp.minimum(z, 0.0) - jnp.log1p(jnp.exp(-jnp.abs(z)))) * (1.0 / GLA_GATE_NORM)
    row = lax.broadcasted_iota(jnp.int32, (c, c), 0)
    col = lax.broadcasted_iota(jnp.int32, (c, c), 1)
    tri = (col >= row) if reverse else (col <= row)
    b = jnp.dot(tri.astype(F32), lg, precision=HI, preferred_element_type=F32)
    i_ref = c // 2 if reverse else c // 2 - 1
    i_last = 0 if reverse else c - 1
    b_ref = b[i_ref:i_ref + 1, :]
    q_in = (q * jnp.exp(b - b_ref)).astype(BF16)
    k_in = (k * jnp.exp(b_ref - b)).astype(BF16)
    q_dec = (q * jnp.exp(b)).astype(BF16)
    bt = b.T
    bt_last = bt[:, i_last:i_last + 1]
    k_dec_t = (k.T * jnp.exp(bt_last - bt)).astype(BF16)
    chunk_decay = jnp.exp(bt_last)
    vb = v.astype(BF16)
    for h in range(GLA_H):
        ks = slice(h * hk, (h + 1) * hk)
        vs = slice(h * hv, (h + 1) * hv)
        s = s_ref[h]
        scores = lax.dot_general(q_in[:, ks], k_in[:, ks], (((1,), (1,)), ((), ())), preferred_element_type=F32)
        scores = jnp.where(tri, scores, 0.0).astype(BF16)
        o_ref[:, vs] = (jnp.dot(scores, vb[:, vs], preferred_element_type=F32)
                        + jnp.dot(q_dec[:, ks], s.astype(BF16), preferred_element_type=F32))
        s_ref[h] = s * chunk_decay[ks] + jnp.dot(k_dec_t[ks], vb[:, vs], preferred_element_type=F32)


def _gla_kernel(scale, *refs):
    (qf_ref, kf_ref, vf_ref, df_ref, qb_ref, kb_ref, vb_ref, db_ref, gk2_ref, gkb_ref, s0_ref,
     of_ref, ob_ref, so_ref, sf, sb) = refs
    ci = pl.program_id(1)

    @pl.when(ci == 0)
    def _():
        sf[...] = s0_ref[0, 0]
        sb[...] = s0_ref[0, 1]

    rank = gk2_ref.shape[1]
    _gla_dir(qf_ref[...] * scale, kf_ref[...], vf_ref[...], df_ref[:, 0:rank], gk2_ref[0], gkb_ref[0],
             sf, of_ref, False)
    _gla_dir(qb_ref[...] * scale, kb_ref[...], vb_ref[...], db_ref[:, rank:2 * rank], gk2_ref[1], gkb_ref[1],
             sb, ob_ref, True)

    @pl.when(ci == pl.num_programs(1) - 1)
    def _():
        so_ref[0, 0] = sf[...]
        so_ref[0, 1] = sb[...]


def _gla_scan(p, gkd, row0, b, l, dk, dv, gk2, gk_b, s0):
    hk, hv = dk // GLA_H, dv // GLA_H
    c = GLA_CHUNK
    n = l // c
    blk0 = row0 // c
    rank = gk2.shape[1]
    assert (2 * dk) % dv == 0

    def tok(bi, ci, rev):
        return blk0 + bi * n + (n - 1 - ci if rev else ci)

    def specs(rev):
        return [pl.BlockSpec((c, dk), lambda bi, ci: (tok(bi, ci, rev), 0)),
                pl.BlockSpec((c, dk), lambda bi, ci: (tok(bi, ci, rev), 1)),
                pl.BlockSpec((c, dv), lambda bi, ci: (tok(bi, ci, rev), 2 * dk // dv)),
                pl.BlockSpec((c, LANES), lambda bi, ci: (tok(bi, ci, rev), 0))]

    def ospec(rev):
        return pl.BlockSpec((c, dv), lambda bi, ci: (bi * n + (n - 1 - ci if rev else ci), 0))

    sspec = pl.BlockSpec((1, 2, GLA_H, hk, hv), lambda bi, ci: (bi, 0, 0, 0, 0))
    return pl.pallas_call(
        functools.partial(_gla_kernel, float(hk) ** -0.5),
        grid=(b, n),
        in_specs=specs(False) + specs(True) + [
            pl.BlockSpec((2, rank, dk), lambda bi, ci: (0, 0, 0)),
            pl.BlockSpec((2, 1, dk), lambda bi, ci: (0, 0, 0)),
            sspec],
        out_specs=[ospec(False), ospec(True), sspec],
        out_shape=[jax.ShapeDtypeStruct((b * l, dv), F32), jax.ShapeDtypeStruct((b * l, dv), F32),
                   jax.ShapeDtypeStruct((b, 2, GLA_H, hk, hv), F32)],
        scratch_shapes=[pltpu.VMEM((GLA_H, hk, hv), F32), pltpu.VMEM((GLA_H, hk, hv), F32)],
        compiler_params=_cparams("parallel", "arbitrary"),
        name="gla_scan",
    )(p, p, p, gkd, p, p, p, gkd, gk2, gk_b.reshape(2, 1, dk), s0)


def _gla_post_kernel(hv, prompt_tiles, ofp_ref, obp_ref, ofs_ref, obs_ref, g_ref, nw_ref, o_ref):
    is_prompt = pl.program_id(0) < prompt_tiles
    o = jnp.where(is_prompt, ofp_ref[...] + obp_ref[...], ofs_ref[...] + obs_ref[...])
    g = g_ref[...]
    for h in range(GLA_H):
        oh = o[:, h * hv:(h + 1) * hv]
        gh = g[:, h * hv:(h + 1) * hv]
        oh = oh * lax.rsqrt(jnp.mean(oh * oh, axis=-1, keepdims=True) + EPS) * nw_ref[...]
        o_ref[:, h * hv:(h + 1) * hv] = (oh * (gh * jax.nn.sigmoid(gh))).astype(o_ref.dtype)


def _gla_post(geo, of_p, ob_p, of_s, ob_s, p, dk, dv, norm_w):
    nt = p.shape[0]
    tm = _pow2_tile(geo.tm, 256)
    ntp = geo.np_ // tm
    hv = dv // GLA_H
    tspec = pl.BlockSpec((tm, dv), lambda i: (i, 0))
    pspec = pl.BlockSpec((tm, dv), lambda i: (jnp.minimum(i, ntp - 1), 0))
    sspec = pl.BlockSpec((tm, dv), lambda i: (jnp.maximum(i - ntp, 0), 0))
    return pl.pallas_call(
        functools.partial(_gla_post_kernel, hv, ntp),
        grid=(nt // tm,),
        in_specs=[pspec, pspec, sspec, sspec, pl.BlockSpec((tm, dv), lambda i: (i, 2 * dk // dv + 1)),
                  pl.BlockSpec((1, hv), lambda i: (0, 0))],
        out_specs=tspec,
        out_shape=jax.ShapeDtypeStruct((nt, dv), BF16),
        compiler_params=_cparams("parallel"),
        name="gla_post",
    )(of_p, ob_p, of_s, ob_s, p, norm_w.reshape(1, hv))


def _gla_mixer(geo, p, gkd, s0_sample, gk2, gk_b, norm_w, dk, dv):
    hk, hv = dk // GLA_H, dv // GLA_H
    s0_p = jnp.zeros((geo.bp, 2, GLA_H, hk, hv), F32)
    of_p, ob_p, s_p = _gla_scan(p, gkd, 0, geo.bp, geo.lp, dk, dv, gk2, gk_b, s0_p)
    of_s, ob_s, _ = _gla_scan(p, gkd, geo.np_, geo.bs, geo.ls, dk, dv, gk2, gk_b, s0_sample)
    return _gla_post(geo, of_p, ob_p, of_s, ob_s, p, dk, dv, norm_w), s_p


def _conv_gate_kernel(geo, tm, g_ref, gp_ref, gn_ref, v_ref, w_ref, o_ref):
    i = pl.program_id(0)
    is_prompt = i * tm < geo.np_
    halo = GRID_W
    n_ext = tm + 2 * halo

    def conv(lseq, ncol):
        ext = jnp.concatenate([gp_ref[...], g_ref[...], gn_ref[...]], axis=0)
        epos = (lax.broadcasted_iota(jnp.int32, (n_ext, 1), 0) + (i * tm - halo)) & (lseq - 1)
        scol = epos & (ncol - 1)
        taps = {-1: pltpu.roll(jnp.where(scol == ncol - 1, 0.0, ext), 1, axis=0),
                0: ext,
                1: pltpu.roll(jnp.where(scol == 0, 0.0, ext), n_ext - 1, axis=0)}

        def row_sum(di):
            start = halo + di * GRID_W
            return sum(taps[dj][start:start + tm, :] * w_ref[di + 1, dj + 1] for dj in (-1, 0, 1))

        acc = row_sum(0)
        nrow = lseq // ncol
        if nrow > 1:
            assert ncol == GRID_W
            tpos = (lax.broadcasted_iota(jnp.int32, (tm, 1), 0) + i * tm) & (lseq - 1)
            row = tpos >> int(math.log2(ncol))
            acc = acc + jnp.where(row >= 1, row_sum(-1), 0.0) + jnp.where(row <= nrow - 2, row_sum(1), 0.0)
        o_ref[...] = (acc * jax.nn.sigmoid(acc) * v_ref[...]).astype(o_ref.dtype)

    @pl.when(is_prompt)
    def _():
        conv(geo.lp, geo.lp)

    @pl.when(jnp.logical_not(is_prompt))
    def _():
        conv(geo.ls, GRID_W)


def _conv_gate(geo, u, conv_w):
    nt = u.shape[0]
    f = u.shape[1] // 2
    tm = _pow2_tile(geo.tm, 256)
    assert tm % GRID_W == 0
    tc = _col_tile(f, 512)
    ncb = f // tc
    r = tm // GRID_W
    last = nt // GRID_W - 1
    return pl.pallas_call(
        functools.partial(_conv_gate_kernel, geo, tm),
        grid=(nt // tm, ncb),
        in_specs=[pl.BlockSpec((tm, tc), lambda i, j: (i, j)),
                  pl.BlockSpec((GRID_W, tc), lambda i, j: (jnp.maximum(i * r - 1, 0), j)),
                  pl.BlockSpec((GRID_W, tc), lambda i, j: (jnp.minimum((i + 1) * r, last), j)),
                  pl.BlockSpec((tm, tc), lambda i, j: (i, ncb + j)),
                  pl.BlockSpec((3, 3, 1, tc), lambda i, j: (0, 0, 0, j))],
        out_specs=pl.BlockSpec((tm, tc), lambda i, j: (i, j)),
        out_shape=jax.ShapeDtypeStruct((nt, f), BF16),
        compiler_params=_cparams("parallel", "arbitrary"),
        name="conv_gate",
    )(u, u, u, u, conv_w.reshape(3, 3, 1, f))


def _pad_cols(w, mult):
    n = w.shape[-1]
    pad = _round_up(n, mult) - n
    return jnp.pad(w, [(0, 0)] * (w.ndim - 1) + [(0, pad)]) if pad else w


def kernel(x_prompt, x_sample, state_rwkv, state_gla, c, c_ctx, w_ada, b_ada, norm_mix, norm_ffn, ffn_w_up, ffn_conv, ffn_w_down, norm_final, ev_w_in, ev_w_out, hy_short_w, hy_short_b, hy_w1, hy_b1, hy_w2, hy_b2, hy_w3, hy_freq, hy_bias, rw_mu, rw_w0, rw_w2, rw_a0, rw_a2, rw_g2, rw_kk, rw_ka, rw_rk, rw_ln_w, rw_ln_b, od_w_in, od_w_out, gla_gk2, gla_gk_b, gla_norm):
    bp, lp, d = x_prompt.shape
    bs, ls, _ = x_sample.shape
    geo = _Geo(bp, lp, bs, ls, d)
    depth = w_ada.shape[0]
    x = jnp.concatenate([x_prompt.reshape(geo.np_, d), x_sample.reshape(geo.ns, d)], axis=0)

    cond = jnp.concatenate([c_ctx[None, :], c], axis=0)
    cond = jnp.pad(cond, ((0, SUBLANES - cond.shape[0]), (0, 0)))
    mods_all = _adaln(cond, w_ada, b_ada)

    rw_states, gla_states = [], []
    for layer in range(depth):
        mods = mods_all[layer].reshape(SUBLANES * 6, 1, d)
        if layer % 2 == 0:
            e = layer // 2
            dh = hy_bias.shape[1]
            dr = rw_kk.shape[1]
            perm = _head_minor_perm(dr)
            perm3 = np.concatenate([perm, dr + perm, 2 * dr + perm, np.arange(3 * dr, ev_w_in.shape[2] - 3 * dh)])
            w_hy = ev_w_in[e][:, :3 * dh].astype(BF16)
            w_rw = _pad_cols(ev_w_in[e][:, 3 * dh:][:, perm3], 512).astype(BF16)
            p_hy = _norm_mm(geo, x, norm_mix[layer], mods, 0, w_hy)
            p_rw = _norm_mm(geo, x, norm_mix[layer], mods, 0, w_rw)
            x0, u = _hy_prep(geo, p_hy, hy_short_w[e], hy_short_b[e])
            filt_p = _hyena_filter(lp, hy_w1[e], hy_b1[e], hy_w2[e], hy_b2[e], hy_w3[e], hy_freq[e], dh)
            filt_s = _hyena_filter(ls, hy_w1[e], hy_b1[e], hy_w2[e], hy_b2[e], hy_w3[e], hy_freq[e], dh)
            y_hy = _hyena_long_conv(geo, x0, u, filt_p, filt_s, hy_bias[e])
            mu = _pad_cols(rw_mu[e][perm3], 512)
            y_rw, s_ctx = _rwkv_mixer(geo, p_rw, state_rwkv[:, e], mu, rw_w0[e][:, perm], rw_w2[e][:, :, perm],
                                      rw_a0[e][:, perm], rw_a2[e][:, :, perm], rw_g2[e][:, perm], rw_kk[e][perm],
                                      rw_ka[e][perm], rw_rk[e].reshape(dr)[perm], rw_ln_w[e][perm], rw_ln_b[e][perm])
            rw_states.append(s_ctx)
            w_out = jnp.concatenate([ev_w_out[e][:dh], ev_w_out[e][dh:][perm]], axis=0).astype(BF16)
            x = _res_mm(geo, (y_hy, y_rw), w_out, x, mods, 2)
        else:
            o = layer // 2
            dk = gla_gk2.shape[3]
            dv = gla_norm.shape[1] * GLA_H
            w_main = od_w_in[o][:, :2 * dk + 2 * dv].astype(BF16)
            w_gk = _pad_cols(od_w_in[o][:, 2 * dk + 2 * dv:], LANES).astype(BF16)
            p = _norm_mm(geo, x, norm_mix[layer], mods, 0, w_main)
            gkd = _norm_mm(geo, x, norm_mix[layer], mods, 0, w_gk)
            y, s_ctx = _gla_mixer(geo, p, gkd, state_gla[:, o], gla_gk2[o], gla_gk_b[o], gla_norm[o], dk, dv)
            gla_states.append(s_ctx)
            x = _res_mm(geo, y, od_w_out[o].astype(BF16), x, mods, 2)
        u_ffn = _norm_mm(geo, x, norm_ffn[layer], mods, 3, ffn_w_up[layer].astype(BF16))
        act = _conv_gate(geo, u_ffn, ffn_conv[layer])
        x = _res_mm(geo, act, ffn_w_down[layer].astype(BF16), x, mods, 5)

    y = _final_norm(geo, x, norm_final)
    y_prompt = y[:geo.np_].reshape(bp, lp, d)
    y_sample = y[geo.np_:].reshape(bs, ls, d)
    return (y_prompt, y_sample, jnp.stack(rw_states, axis=1), jnp.stack(gla_states, axis=1))
```

```python
import functools
import math

import numpy as np
import jax
import jax.numpy as jnp
from jax import lax
from jax.experimental import pallas as pl
from jax.experimental.pallas import tpu as pltpu

F32 = jnp.float32
BF16 = jnp.bfloat16
HI = lax.Precision.HIGHEST

EPS = 1e-6
RW_HEAD = 64
RW_GN_EPS = 64e-5
GLA_H = 4
GLA_GATE_NORM = 16.0
GLA_CHUNK = 64
GRID_W = 64
HY_TARGET = 1e-2
HY_FAST = 0.3
HY_SLOW = 1.5

LANES = 128
SUBLANES = 8
MXU_DIM = 256
VMEM_LIMIT = 56 * 1024 * 1024
FFT_MINOR = 128


def _cparams(*sem):
    return pltpu.CompilerParams(dimension_semantics=sem, vmem_limit_bytes=VMEM_LIMIT)


def _round_up(n, m):
    return (n + m - 1) // m * m


def _pow2_tile(n, pref):
    t = 1
    while t * 2 <= pref and n % (t * 2) == 0:
        t *= 2
    return t


def _col_tile(n, pref):
    if n % LANES:
        return n
    divisors = [t for t in range(LANES, min(n, pref) + 1, LANES) if n % t == 0]
    mxu_wide = [t for t in divisors if t % MXU_DIM == 0]
    return max(mxu_wide or divisors)


class _Geo:
    def __init__(self, bp, lp, bs, ls, d):
        self.bp, self.lp, self.bs, self.ls, self.d = bp, lp, bs, ls, d
        self.np_, self.ns = bp * lp, bs * ls
        self.nt = self.np_ + self.ns
        self.tm = _pow2_tile(math.gcd(self.np_, ls), 512)
        assert lp & (lp - 1) == 0 and ls & (ls - 1) == 0, "sequence lengths must be powers of two"
        assert self.np_ % ls == 0, "sample rows must start on a sequence-length boundary"
        assert bs + 1 <= SUBLANES

    def row_tile(self, pref):
        return _pow2_tile(math.gcd(self.np_, self.ls), pref)

    def modrow(self, i, tm):
        r0 = i * tm
        return jnp.where(r0 < self.np_, 0, 1 + (r0 - self.np_) // self.ls)

    def seq_len(self, i, tm):
        return jnp.where(i * tm < self.np_, self.lp, self.ls)

    def mirror_tile(self, i, tm):
        def mirrored(tiles_per_seq):
            return (i // tiles_per_seq) * tiles_per_seq + (tiles_per_seq - 1 - i % tiles_per_seq)
        return jnp.where(i * tm < self.np_, mirrored(self.lp // tm), mirrored(self.ls // tm))

    @staticmethod
    def streams_per_group(dr):
        return LANES // (dr // RW_HEAD)

    def stream_slot(self, i, tm, direction, dr):
        r0 = i * tm
        batch = jnp.where(r0 < self.np_, r0 // self.lp, (r0 - self.np_) // self.ls)
        nbatch = jnp.where(r0 < self.np_, self.bp, self.bs)
        return (direction * nbatch + batch) % self.streams_per_group(dr)


def _adaln_kernel(c_ref, w_ref, b_ref, o_ref):
    c = c_ref[...]
    s = c * jax.nn.sigmoid(c)
    o_ref[0] = jnp.dot(s, w_ref[0], precision=HI, preferred_element_type=F32) + b_ref[0]


def _adaln(cond, w_ada, b_ada):
    depth, d, n = w_ada.shape
    tn = _col_tile(n, 1024)
    return pl.pallas_call(
        _adaln_kernel,
        grid=(depth, n // tn),
        in_specs=[pl.BlockSpec((SUBLANES, d), lambda l, j: (0, 0)),
                  pl.BlockSpec((1, d, tn), lambda l, j: (l, 0, j)),
                  pl.BlockSpec((1, 1, tn), lambda l, j: (l, 0, j))],
        out_specs=pl.BlockSpec((1, SUBLANES, tn), lambda l, j: (l, 0, j)),
        out_shape=jax.ShapeDtypeStruct((depth, SUBLANES, n), F32),
        compiler_params=_cparams("parallel", "arbitrary"),
        name="adaln",
    )(cond, w_ada, b_ada.reshape(depth, 1, n))


def _norm_mm_kernel(x_ref, g_ref, sh_ref, sc_ref, w_ref, o_ref, xn_ref):
    @pl.when(pl.program_id(1) == 0)
    def _():
        x = x_ref[...]
        ms = jnp.mean(x * x, axis=-1, keepdims=True)
        y = x * lax.rsqrt(ms + EPS) * g_ref[...]
        xn_ref[...] = (y * (1.0 + sc_ref[0]) + sh_ref[0]).astype(BF16)

    o_ref[...] = jnp.dot(xn_ref[...], w_ref[...], preferred_element_type=F32).astype(o_ref.dtype)


def _norm_mm(geo, x, g, mods, which_shift, w, out_dtype=F32, tm_pref=1024, tn_pref=1024):
    nt, d = x.shape
    n = w.shape[1]
    tm = geo.row_tile(tm_pref)
    tn = _col_tile(n, tn_pref)

    def mod_map(which):
        return lambda i, j: (geo.modrow(i, tm) * 6 + which, 0, 0)

    return pl.pallas_call(
        _norm_mm_kernel,
        grid=(nt // tm, n // tn),
        in_specs=[pl.BlockSpec((tm, d), lambda i, j: (i, 0)),
                  pl.BlockSpec((1, d), lambda i, j: (0, 0)),
                  pl.BlockSpec((1, 1, d), mod_map(which_shift)),
                  pl.BlockSpec((1, 1, d), mod_map(which_shift + 1)),
                  pl.BlockSpec((d, tn), lambda i, j: (0, j))],
        out_specs=pl.BlockSpec((tm, tn), lambda i, j: (i, j)),
        out_shape=jax.ShapeDtypeStruct((nt, n), out_dtype),
        scratch_shapes=[pltpu.VMEM((tm, d), BF16)],
        compiler_params=_cparams("parallel", "arbitrary"),
        name="norm_mm",
    )(x, g.reshape(1, d), mods, mods, w)


def _res_mm_kernel(a_ref, w_ref, res_ref, gt_ref, o_ref):
    acc = jnp.dot(a_ref[...], w_ref[...], preferred_element_type=F32)
    o_ref[...] = res_ref[...] + gt_ref[0] * acc


def _res_mm2_kernel(a1_ref, a2_ref, w_ref, res_ref, gt_ref, o_ref):
    k1 = a1_ref.shape[1]
    acc = jnp.dot(a1_ref[...], w_ref[0:k1, :], preferred_element_type=F32)
    acc = acc + jnp.dot(a2_ref[...], w_ref[k1:, :], preferred_element_type=F32)
    o_ref[...] = res_ref[...] + gt_ref[0] * acc


def _res_mm(geo, a, w, res, mods, which_gate, tm_pref=1024):
    parts = a if isinstance(a, (tuple, list)) else (a,)
    nt = parts[0].shape[0]
    k = sum(p.shape[1] for p in parts)
    n = w.shape[1]
    tm = geo.row_tile(tm_pref)
    tn = _col_tile(n, 1024 if k <= 2048 else 512)
    return pl.pallas_call(
        _res_mm_kernel if len(parts) == 1 else _res_mm2_kernel,
        grid=(nt // tm, n // tn),
        in_specs=[pl.BlockSpec((tm, p.shape[1]), lambda i, j: (i, 0)) for p in parts] + [
                  pl.BlockSpec((k, tn), lambda i, j: (0, j)),
                  pl.BlockSpec((tm, tn), lambda i, j: (i, j)),
                  pl.BlockSpec((1, 1, tn), lambda i, j: (geo.modrow(i, tm) * 6 + which_gate, 0, j))],
        out_specs=pl.BlockSpec((tm, tn), lambda i, j: (i, j)),
        out_shape=jax.ShapeDtypeStruct((nt, n), F32),
        compiler_params=_cparams("parallel", "arbitrary"),
        name="res_mm",
    )(*parts, w, res, mods)


def _final_norm_kernel(x_ref, g_ref, o_ref):
    x = x_ref[...]
    ms = jnp.mean(x * x, axis=-1, keepdims=True)
    o_ref[...] = x * lax.rsqrt(ms + EPS) * g_ref[...]


def _final_norm(geo, x, g):
    nt, d = x.shape
    tm = geo.tm
    return pl.pallas_call(
        _final_norm_kernel,
        grid=(nt // tm,),
        in_specs=[pl.BlockSpec((tm, d), lambda i: (i, 0)), pl.BlockSpec((1, d), lambda i: (0, 0))],
        out_specs=pl.BlockSpec((tm, d), lambda i: (i, 0)),
        out_shape=jax.ShapeDtypeStruct((nt, d), F32),
        compiler_params=_cparams("parallel"),
        name="final_norm",
    )(x, g.reshape(1, d))


def _seq_pos(geo, i, tm):
    lseq = geo.seq_len(i, tm)
    rid = lax.broadcasted_iota(jnp.int32, (tm, 1), 0) + i * tm
    return rid & (lseq - 1), lseq


def _neighbours(x, prev_row, next_row, tpos, lseq):
    tm = x.shape[0]
    rid = lax.broadcasted_iota(jnp.int32, (tm, 1), 0)
    xm = pltpu.roll(x, 1, axis=0)
    xm = jnp.where(rid == 0, prev_row, xm)
    xm = jnp.where(tpos == 0, 0.0, xm)
    xp = pltpu.roll(x, tm - 1, axis=0)
    xp = jnp.where(rid == tm - 1, next_row, xp)
    xp = jnp.where(tpos == lseq - 1, 0.0, xp)
    return xm, xp


def _halo_specs(nt, tm, width, col):
    r = tm // SUBLANES
    last = nt // SUBLANES - 1
    prev = pl.BlockSpec((SUBLANES, width), lambda i, j: (jnp.maximum(i * r - 1, 0), col(j)))
    nxt = pl.BlockSpec((SUBLANES, width), lambda i, j: (jnp.minimum((i + 1) * r, last), col(j)))
    return prev, nxt


def _hy_prep_kernel(geo, tm, *refs):
    (x0_ref, x0p_ref, x0n_ref, x1_ref, x1p_ref, x1n_ref, x2_ref, x2p_ref, x2n_ref,
     w_ref, b_ref, o0_ref, u_ref) = refs
    i = pl.program_id(0)
    tpos, lseq = _seq_pos(geo, i, tm)

    def conv(x_ref, p_ref, n_ref, g):
        x = x_ref[...]
        xm, xp = _neighbours(x, p_ref[SUBLANES - 1:SUBLANES, :], n_ref[0:1, :], tpos, lseq)
        return xm * w_ref[0, g] + x * w_ref[1, g] + xp * w_ref[2, g] + b_ref[g]

    c0 = conv(x0_ref, x0p_ref, x0n_ref, 0)
    c1 = conv(x1_ref, x1p_ref, x1n_ref, 1)
    c2 = conv(x2_ref, x2p_ref, x2n_ref, 2)
    o0_ref[...] = c0
    u_ref[...] = c1 * c2


def _hy_prep(geo, p_hy, short_w, short_b):
    nt = p_hy.shape[0]
    dh = p_hy.shape[1] // 3
    tm = _pow2_tile(geo.tm, 256)
    tc = _col_tile(dh, 512)
    ncb = dh // tc
    specs = []
    for g in range(3):
        col = (lambda j, g=g: g * ncb + j)
        specs.append(pl.BlockSpec((tm, tc), lambda i, j, col=col: (i, col(j))))
        specs.extend(_halo_specs(nt, tm, tc, col))
    specs.append(pl.BlockSpec((3, 3, 1, tc), lambda i, j: (0, 0, 0, j)))
    specs.append(pl.BlockSpec((3, 1, tc), lambda i, j: (0, 0, j)))
    out_spec = pl.BlockSpec((tm, tc), lambda i, j: (i, j))
    w4 = short_w.reshape(3, 3, 1, dh)
    b3 = short_b.reshape(3, 1, dh)
    return pl.pallas_call(
        functools.partial(_hy_prep_kernel, geo, tm),
        grid=(nt // tm, ncb),
        in_specs=specs,
        out_specs=[out_spec, out_spec],
        out_shape=[jax.ShapeDtypeStruct((nt, dh), F32)] * 2,
        compiler_params=_cparams("parallel", "arbitrary"),
        name="hy_prep",
    )(*([p_hy] * 9), w4, b3)


def _lmm_kernel(has_epi, *refs):
    if has_epi:
        a_ref, x_ref, x0_ref, u_ref, bias_ref, o_ref = refs
    else:
        a_ref, x_ref, o_ref = refs
    acc = jnp.dot(a_ref[0], x_ref[0], precision=HI, preferred_element_type=F32)
    if has_epi:
        acc = x0_ref[0] * (acc + u_ref[0] * bias_ref[...])
    o_ref[0] = acc.astype(o_ref.dtype)


def _lmm(a, x, epi=None, out_dtype=F32):
    gm, m, k = a.shape
    g, _, n = x.shape
    tn = _col_tile(n, 2048)
    xspec = pl.BlockSpec((1, k, tn), lambda gi, j: (gi, 0, j))
    ospec = pl.BlockSpec((1, m, tn), lambda gi, j: (gi, 0, j))
    specs = [pl.BlockSpec((1, m, k), lambda gi, j: (gi % gm, 0, 0)), xspec]
    args = [a, x]
    if epi is not None:
        x0, u, bias = epi
        specs += [ospec, ospec, pl.BlockSpec((1, tn), lambda gi, j: (0, j))]
        args += [x0, u, bias]
    return pl.pallas_call(
        functools.partial(_lmm_kernel, epi is not None),
        grid=(g, n // tn),
        in_specs=specs,
        out_specs=ospec,
        out_shape=jax.ShapeDtypeStruct((g, m, n), out_dtype),
        compiler_params=_cparams("parallel", "arbitrary"),
        name="dft_lmm",
    )(*args)


def _spec_kernel(has_epi, *refs):
    if has_epi:
        mf_ref, h_ref, mi_ref, x_ref, x0_ref, bias_ref, o_ref = refs
    else:
        mf_ref, h_ref, mi_ref, x_ref, o_ref = refs
    x = x_ref[0]
    f = jnp.dot(mf_ref[0], x, precision=HI, preferred_element_type=F32)
    r = f.shape[0] // 2
    fr, fi = f[:r], f[r:]
    hr, hi = h_ref[0, :r], h_ref[0, r:]
    y = jnp.concatenate([fr * hr - fi * hi, fr * hi + fi * hr], axis=0)
    out = jnp.dot(mi_ref[0], y, precision=HI, preferred_element_type=F32)
    if has_epi:
        out = x0_ref[0] * (out + x * bias_ref[...])
    o_ref[0] = out.astype(o_ref.dtype)


def _spectral(mf, h, mi, x, epi=None, out_dtype=F32):
    gm, r2, k = mf.shape
    kout = mi.shape[1]
    g, _, c = x.shape
    tc = _col_tile(c, 512)
    xspec = pl.BlockSpec((1, k, tc), lambda gi, j: (gi, 0, j))
    ospec = pl.BlockSpec((1, kout, tc), lambda gi, j: (gi, 0, j))
    specs = [pl.BlockSpec((1, r2, k), lambda gi, j: (gi % gm, 0, 0)),
             pl.BlockSpec((1, r2, tc), lambda gi, j: (gi % gm, 0, j)),
             pl.BlockSpec((1, kout, r2), lambda gi, j: (gi % gm, 0, 0)),
             xspec]
    args = [mf, h, mi, x]
    if epi is not None:
        x0, bias = epi
        specs += [ospec, pl.BlockSpec((1, tc), lambda gi, j: (0, j))]
        args += [x0, bias]
    return pl.pallas_call(
        functools.partial(_spec_kernel, epi is not None),
        grid=(g, c // tc),
        in_specs=specs,
        out_specs=ospec,
        out_shape=jax.ShapeDtypeStruct((g, kout, c), out_dtype),
        compiler_params=_cparams("parallel", "arbitrary"),
        name="dft_spectral",
    )(*args)


def _dft_tables_direct(l):
    n = 2 * l
    nf = l + 1
    r = _round_up(nf, SUBLANES)
    kk = np.arange(nf)[:, None].astype(np.float64)
    ang_half = 2.0 * np.pi * ((kk * np.arange(l)[None, :]) % n) / n
    ang_full = 2.0 * np.pi * ((kk * np.arange(n)[None, :]) % n) / n

    def fwd(ang):
        m = np.zeros((2 * r, ang.shape[1]))
        m[:nf] = np.cos(ang)
        m[r:r + nf] = -np.sin(ang)
        return m

    coef = np.full((nf,), 2.0 / n)
    coef[0] = coef[-1] = 1.0 / n
    inv = np.zeros((l, 2 * r))
    inv[:, :nf] = (np.cos(ang_half) * coef[:, None]).T
    inv[:, r:r + nf] = (-np.sin(ang_half) * coef[:, None]).T
    f32 = lambda a: jnp.asarray(a[None], F32)
    return f32(fwd(ang_half)), f32(fwd(ang_full)), f32(inv)


def _dft_tables_two_level(l):
    n = 2 * l
    n1 = FFT_MINOR
    n2 = n // n1
    nb = n2 // 2 + 1
    nbp = _round_up(nb, 4)
    k2 = np.arange(nb)[:, None].astype(np.float64)
    ang2 = 2.0 * np.pi * ((k2 * np.arange(n2)[None, :]) % n2) / n2
    ms_full = np.zeros((2 * nbp, n2))
    ms_full[0:2 * nb:2] = np.cos(ang2)
    ms_full[1:2 * nb:2] = -np.sin(ang2)
    ms_half = ms_full[:, :n2 // 2]
    coef = np.full((nb,), 2.0 / n)
    coef[0] = coef[-1] = 1.0 / n
    ms_inv = np.zeros((n2 // 2, 2 * nbp))
    ms_inv[:, 0:2 * nb:2] = (np.cos(ang2[:, :n2 // 2]) * coef[:, None]).T
    ms_inv[:, 1:2 * nb:2] = (-np.sin(ang2[:, :n2 // 2]) * coef[:, None]).T

    a_fwd = np.zeros((nbp, 2 * n1, 2 * n1))
    a_inv = np.zeros((nbp, 2 * n1, 2 * n1))
    i1 = np.arange(n1).astype(np.float64)
    for b in range(nb):
        ph = 2.0 * np.pi * (((np.outer(i1, i1) * n2) + (i1[None, :] * b)) % n) / n
        wr, wi = np.cos(ph), -np.sin(ph)
        a_fwd[b] = np.block([[wr, -wi], [wi, wr]])
        ph = 2.0 * np.pi * (((np.outer(i1, i1) * n2) + (i1[:, None] * b)) % n) / n
        pr, pi_ = np.cos(ph), np.sin(ph)
        a_inv[b] = np.block([[pr, -pi_], [pi_, pr]])
    f32 = lambda a: jnp.asarray(a, F32)
    return (f32(ms_half[None]), f32(ms_full[None]), f32(ms_inv[None]), f32(a_fwd), f32(a_inv), n1, n2, nbp)


def _hyena_filter(l, w1, b1, w2, b2, w3, freq, dh):
    rows = jnp.arange(2 * l)[:, None]
    pos = jnp.where(rows < l, rows, 2 * l - rows).astype(F32)
    t = pos / max(l - 1, 1)
    emb = w1.shape[0]
    bands = (emb - 1) // 2
    fb = jnp.linspace(1e-4, bands - 1, bands, dtype=F32)
    ang = (2.0 * math.pi / l) * pos * fb
    z = jnp.concatenate([t, jnp.cos(ang), -jnp.sin(ang)], axis=-1)
    h = jnp.sin(freq[0] * (jnp.dot(z, w1, precision=HI) + b1))
    h = jnp.sin(freq[1] * (jnp.dot(h, w2, precision=HI) + b2))
    h = jnp.dot(h, w3, precision=HI)
    deltas = jnp.abs(jnp.linspace(math.log(HY_TARGET) / HY_SLOW, math.log(HY_TARGET) / HY_FAST, dh, dtype=F32))
    half = jnp.where(rows < l, h[:, :dh], h[:, dh:])
    filt = jnp.where(rows == l, 0.0, half * jnp.exp(-t * deltas))
    return filt / jnp.sum(jnp.abs(filt), axis=0, keepdims=True)


def _hyena_long_conv(geo, x0, u, filt_p, filt_s, bias):
    dh = u.shape[1]
    bias2 = bias.reshape(1, dh)
    mf_half, mf_full, mi = _dft_tables_direct(geo.lp)
    h_p = _lmm(mf_full, filt_p[None])
    up = u[:geo.np_].reshape(geo.bp, geo.lp, dh)
    x0p = x0[:geo.np_].reshape(geo.bp, geo.lp, dh)
    y_p = _spectral(mf_half, h_p, mi, up, epi=(x0p, bias2), out_dtype=BF16)
    ms_half, ms_full, ms_inv, a_fwd, a_inv, n1, n2, nbp = _dft_tables_two_level(geo.ls)
    hs = _lmm(ms_full, filt_s.reshape(1, n2, n1 * dh))
    hs = _lmm(a_fwd, hs.reshape(nbp, 2 * n1, dh))
    us = u[geo.np_:].reshape(geo.bs, n2 // 2, n1 * dh)
    x0s = x0[geo.np_:].reshape(geo.bs, n2 // 2, n1 * dh)
    xs = _lmm(ms_half, us)
    zs = _spectral(a_fwd, hs, a_inv, xs.reshape(geo.bs * nbp, 2 * n1, dh))
    zs = zs.reshape(geo.bs, 2 * nbp, n1 * dh)
    y_s = _lmm(ms_inv, zs, epi=(x0s, us, jnp.tile(bias2, (1, n1))), out_dtype=BF16)
    return jnp.concatenate([y_p.reshape(geo.np_, dh), y_s.reshape(geo.ns, dh)], axis=0)


def _head_sum(x):
    dr = x.shape[1]
    nheads = dr // RW_HEAD
    assert LANES % nheads == 0
    t = x[:, 0:LANES]
    for q in range(1, dr // LANES):
        t = t + x[:, q * LANES:(q + 1) * LANES]
    shift = LANES // 2
    while shift >= nheads:
        t = t + pltpu.roll(t, shift, axis=1)
        shift //= 2
    return jnp.concatenate([t] * (dr // LANES), axis=1)


def _flip_rows(x):
    tm, c = x.shape
    x3 = x.reshape(tm // SUBLANES, SUBLANES, c)
    rid = lax.broadcasted_iota(jnp.int32, (1, SUBLANES, 1), 1)
    for s in (4, 2, 1):
        x3 = jnp.where((rid & s) != 0, pltpu.roll(x3, s, axis=1), pltpu.roll(x3, SUBLANES - s, axis=1))
    return jnp.concatenate([x3[j] for j in reversed(range(tm // SUBLANES))], axis=0)


def _rw_prep_kernel(geo, tm, dr, *refs):
    (x_ref, xp_ref, xn_ref, mu_ref, w0_ref, w2_ref, a0_ref, a2_ref, g2_ref, kk_ref, ka_ref, rk_ref,
     rf_o, nkkf_o, vf_o, wf_o, kxf_o, bf_o, rb_o, nkkb_o, vb_o, wb_o, kxb_o, bb_o, g_o, bon_o) = refs
    i = pl.program_id(0)
    tpos, lseq = _seq_pos(geo, i, tm)
    x = x_ref[...]
    xm, xp = _neighbours(x, xp_ref[SUBLANES - 1:SUBLANES, :], xn_ref[0:1, :], tpos, lseq)
    x = x + mu_ref[...] * (0.5 * (xm + xp) - x)
    r = x[:, 0:dr]
    k = x[:, dr:2 * dr]
    v = x[:, 2 * dr:3 * dr]
    wd = x[:, 3 * dr:3 * dr + LANES]
    ad = x[:, 3 * dr + LANES:3 * dr + 2 * LANES]
    gd = x[:, 3 * dr + 2 * LANES:3 * dr + 3 * LANES]
    wl = w0_ref[...] + jnp.dot(jnp.tanh(wd), w2_ref[...], precision=HI, preferred_element_type=F32)
    w_log = -jax.nn.softplus(-wl) - 0.5
    decay = jnp.exp(-jnp.exp(w_log))
    a = jax.nn.sigmoid(a0_ref[...] + jnp.dot(ad, a2_ref[...], precision=HI, preferred_element_type=F32))
    g_o[...] = jnp.dot(jax.nn.sigmoid(gd), g2_ref[...], precision=HI, preferred_element_type=F32)
    kk = k * kk_ref[...]
    kk = kk * lax.rsqrt(_head_sum(kk * kk) + 1e-12)
    a_0, a_1 = a[:, :dr], a[:, dr:]
    kx0 = k * (1.0 + (a_0 - 1.0) * ka_ref[...])
    kx1 = k * (1.0 + (a_1 - 1.0) * ka_ref[...])
    bon_o[...] = _head_sum(r * (kx0 + kx1) * rk_ref[...]) * v

    slab = LANES // geo.streams_per_group(dr)

    def emit(o_ref, val, direction):
        if direction:
            val = _flip_rows(val)
        shift = geo.stream_slot(i, tm, direction, dr) * slab
        for q in range(dr // LANES):
            o_ref[:, q * LANES:(q + 1) * LANES] = pltpu.roll(val[:, q * LANES:(q + 1) * LANES], shift, axis=1)

    nkk = -kk
    for o_f, o_b, val_f, val_b in ((rf_o, rb_o, r, r), (nkkf_o, nkkb_o, nkk, nkk), (vf_o, vb_o, v, v),
                                   (wf_o, wb_o, decay[:, :dr], decay[:, dr:]), (kxf_o, kxb_o, kx0, kx1),
                                   (bf_o, bb_o, kk * a_0, kk * a_1)):
        emit(o_f, val_f, 0)
        emit(o_b, val_b, 1)


def _blockdiag2(w):
    _, r, c = w.shape
    z = jnp.zeros((r, c), w.dtype)
    return jnp.concatenate([jnp.concatenate([w[0], z], axis=1), jnp.concatenate([z, w[1]], axis=1)], axis=0)


def _head_minor_perm(dr):
    return np.arange(dr).reshape(dr // RW_HEAD, RW_HEAD).T.reshape(-1)


def _rw_prep(geo, p_rw, mu, w0, w2, a0, a2, g2, k_k, k_a, r_k):
    nt, width = p_rw.shape
    dr = k_k.shape[0]
    assert 2 * w2.shape[1] == LANES and 2 * a2.shape[1] == LANES and g2.shape[0] == LANES
    tm = _pow2_tile(min(geo.tm, geo.lp), 128)
    full = lambda shape: pl.BlockSpec(shape, lambda i, j: tuple(0 for _ in shape))
    prev, nxt = _halo_specs(nt, tm, width, lambda j: 0)
    consts = [mu.reshape(1, width), w0.reshape(1, 2 * dr), _blockdiag2(w2), a0.reshape(1, 2 * dr),
              _blockdiag2(a2), g2, k_k.reshape(1, dr), k_a.reshape(1, dr), r_k.reshape(1, dr)]
    ospec = pl.BlockSpec((tm, dr), lambda i, j: (i, 0))
    mspec = pl.BlockSpec((tm, dr), lambda i, j: (geo.mirror_tile(i, tm), 0))
    return pl.pallas_call(
        functools.partial(_rw_prep_kernel, geo, tm, dr),
        grid=(nt // tm, 1),
        in_specs=[pl.BlockSpec((tm, width), lambda i, j: (i, 0)), prev, nxt] + [full(c.shape) for c in consts],
        out_specs=[ospec] * 6 + [mspec] * 6 + [ospec] * 2,
        out_shape=[jax.ShapeDtypeStruct((nt, dr), F32)] * 14,
        compiler_params=_cparams("parallel", "arbitrary"),
        name="rw_prep",
    )(p_rw, p_rw, p_rw, *consts)


def _rw_scan_kernel(tc, r_ref, nkk_ref, v_ref, w_ref, kx_ref, b_ref, s0_ref, y_ref, so_ref, s_ref):
    t_chunk = pl.program_id(1)

    @pl.when(t_chunk == 0)
    def _():
        s_ref[...] = s0_ref[...]

    nacc = 2
    vhalf = RW_HEAD // 2

    def row(ref, t, k):
        return ref[t // SUBLANES, k, pl.ds(t % SUBLANES, 1), :]

    def state_dot_neg_kk(t):
        acc = [jnp.zeros((RW_HEAD, LANES), F32) for _ in range(nacc)]
        for k in range(RW_HEAD):
            acc[k % nacc] = acc[k % nacc] + s_ref[k] * row(nkk_ref, t, k)
        return acc[0] + acc[1]

    def step(t, sa):
        t_next = jnp.minimum(t + 1, tc - 1)
        sa_next = []
        for half in range(2):
            vs = slice(half * vhalf, (half + 1) * vhalf)
            sa_h = sa[vs]
            vv = v_ref[t, vs, :]
            acc_y = [jnp.zeros((vhalf, LANES), F32) for _ in range(nacc)]
            acc_s = [jnp.zeros((vhalf, LANES), F32) for _ in range(nacc)]
            for k in range(RW_HEAD):
                new = s_ref[k, vs, :] * row(w_ref, t, k) + sa_h * row(b_ref, t, k) + vv * row(kx_ref, t, k)
                s_ref[k, vs, :] = new
                acc_y[k % nacc] = acc_y[k % nacc] + new * row(r_ref, t, k)
                acc_s[k % nacc] = acc_s[k % nacc] + new * row(nkk_ref, t_next, k)
            y_ref[t, vs, :] = acc_y[0] + acc_y[1]
            sa_next.append(acc_s[0] + acc_s[1])
        return jnp.concatenate(sa_next, axis=0)

    lax.fori_loop(0, tc, step, state_dot_neg_kk(0))

    @pl.when(t_chunk == pl.num_programs(1) - 1)
    def _():
        so_ref[...] = s_ref[...]


def _rw_scan(r, nkk, v, w, kx, b, s0):
    t_len, n, lanes = v.shape
    tc = _pow2_tile(t_len, 32)
    xspec = pl.BlockSpec((tc, n, LANES), lambda g, t: (t, 0, g))
    rspec = pl.BlockSpec((tc // SUBLANES, n, SUBLANES, LANES), lambda g, t: (t, 0, 0, g))
    sspec = pl.BlockSpec((n, n, LANES), lambda g, t: (0, 0, g))
    return pl.pallas_call(
        functools.partial(_rw_scan_kernel, tc),
        grid=(lanes // LANES, t_len // tc),
        in_specs=[rspec, rspec, xspec, rspec, rspec, rspec, sspec],
        out_specs=[xspec, sspec],
        out_shape=[jax.ShapeDtypeStruct((t_len, n, lanes), F32), jax.ShapeDtypeStruct((n, n, lanes), F32)],
        scratch_shapes=[pltpu.VMEM((n, n, LANES), F32)],
        compiler_params=_cparams("parallel", "arbitrary"),
        name="rw_scan",
    )(r, nkk, v, w, kx, b, s0)


def _merge_slabs(pieces, slab, offset):
    n = len(pieces)
    lane_slab = lax.broadcasted_iota(jnp.int32, pieces[0].shape, 1) // slab
    acc = pieces[n - 1]
    for p in range(n - 1):
        acc = jnp.where(lane_slab == (offset + p) % n, pieces[p], acc)
    return acc


def _to_streams_kernel(nsrc, ntens, slab, value_index, *refs):
    srcs, outs = refs[:ntens * nsrc], refs[ntens * nsrc:]
    for ti in range(ntens):
        tiles = [srcs[ti * nsrc + s][...] for s in range(nsrc)]
        tt = tiles[0].shape[0]
        for j in range(SUBLANES):
            merged = _merge_slabs(tiles, slab, j)
            shift = ((nsrc - j) % nsrc) * slab
            merged = pltpu.roll(merged, shift, axis=1) if shift else merged
            if ti == value_index:
                outs[ti][:, j, :] = merged
            else:
                outs[ti][:, j] = merged.reshape(tt // SUBLANES, SUBLANES, LANES)


def _to_streams(fwd, bwd, row0, nbatch, l, direction):
    ntens = len(fwd)
    dr = fwd[0].shape[1]
    nsrc = _Geo.streams_per_group(dr)
    assert nsrc == SUBLANES, "the stream layout assumes 16 heads (eight channels per 128-lane tile)"
    slab = LANES // nsrc
    tt = _pow2_tile(l, 128)
    if direction is None:
        assert 2 * nbatch == nsrc
        ngroups = 1
        source = lambda g, s: (s // nbatch, s % nbatch)
    else:
        assert nbatch % nsrc == 0
        ngroups = nbatch // nsrc
        source = lambda g, s: (direction, g * nsrc + s)
    specs, args = [], []
    for ti in range(ntens):
        for s in range(nsrc):
            d = source(0, s)[0]
            specs.append(pl.BlockSpec(
                (tt, LANES), lambda g, i, q, s=s: ((row0 + source(g, s)[1] * l) // tt + i, q)))
            args.append((bwd if d else fwd)[ti])
    value_index = 2
    vspec = pl.BlockSpec((tt, SUBLANES, LANES), lambda g, i, q: (i, q, g))
    rspec = pl.BlockSpec((tt // SUBLANES, SUBLANES, SUBLANES, LANES), lambda g, i, q: (i, q, 0, g))
    vshape = jax.ShapeDtypeStruct((l, RW_HEAD, ngroups * LANES), F32)
    rshape = jax.ShapeDtypeStruct((l // SUBLANES, RW_HEAD, SUBLANES, ngroups * LANES), F32)
    return pl.pallas_call(
        functools.partial(_to_streams_kernel, nsrc, ntens, slab, value_index),
        grid=(ngroups, l // tt, dr // LANES),
        in_specs=specs,
        out_specs=[vspec if ti == value_index else rspec for ti in range(ntens)],
        out_shape=[vshape if ti == value_index else rshape for ti in range(ntens)],
        compiler_params=_cparams("parallel", "parallel", "arbitrary"),
        name="to_streams",
    )(*args)


def _from_streams_kernel(nsrc, slab, y_ref, o_ref):
    rolled = [y_ref[:, j, :] if j == 0 else pltpu.roll(y_ref[:, j, :], j * slab, axis=1) for j in range(SUBLANES)]
    for s in range(nsrc):
        merged = _merge_slabs(rolled, slab, s)
        shift = ((nsrc - s) % nsrc) * slab
        o_ref[s] = pltpu.roll(merged, shift, axis=1) if shift else merged


def _from_streams(y, dr):
    l, _, lanes = y.shape
    nsrc = _Geo.streams_per_group(dr)
    slab = LANES // nsrc
    tt = _pow2_tile(l, 128)
    ngroups = lanes // LANES
    return pl.pallas_call(
        functools.partial(_from_streams_kernel, nsrc, slab),
        grid=(ngroups, l // tt, dr // LANES),
        in_specs=[pl.BlockSpec((tt, SUBLANES, LANES), lambda g, i, q: (i, q, g))],
        out_specs=pl.BlockSpec((nsrc, tt, LANES), lambda g, i, q: (g, i, q)),
        out_shape=jax.ShapeDtypeStruct((ngroups * nsrc, l, dr), F32),
        compiler_params=_cparams("parallel", "parallel", "arbitrary"),
        name="from_streams",
    )(y)


def _rw_post_kernel(prompt_tiles, yfp_ref, ybp_ref, yfs_ref, ybs_ref, bon_ref, g_ref, lnw_ref, lnb_ref, o_ref):
    is_prompt = pl.program_id(0) < prompt_tiles
    yf = jnp.where(is_prompt, yfp_ref[...], yfs_ref[...])
    yb = jnp.where(is_prompt, ybp_ref[...], ybs_ref[...])
    y = yf + _flip_rows(yb)
    inv_n = 1.0 / RW_HEAD
    mean = _head_sum(y) * inv_n
    yc = y - mean
    var = _head_sum(yc * yc) * inv_n
    yn = yc * lax.rsqrt(var + RW_GN_EPS) * lnw_ref[...] + lnb_ref[...]
    o_ref[...] = ((yn + bon_ref[...]) * g_ref[...]).astype(o_ref.dtype)


def _rw_post(geo, yf_p, yb_p, y_s, bonus, g, ln_w, ln_b):
    nt, dr = bonus.shape
    tm = _pow2_tile(min(geo.tm, geo.lp), 256)
    ntp, nts = geo.np_ // tm, geo.ns // tm
    tspec = pl.BlockSpec((tm, dr), lambda i: (i, 0))
    specs = [pl.BlockSpec((tm, dr), lambda i: (jnp.minimum(i, ntp - 1), 0)),
             pl.BlockSpec((tm, dr), lambda i: (jnp.where(i < ntp, geo.mirror_tile(i, tm), 0), 0)),
             pl.BlockSpec((tm, dr), lambda i: (jnp.maximum(i - ntp, 0), 0)),
             pl.BlockSpec((tm, dr), lambda i: (nts + jnp.where(i < ntp, 0, geo.mirror_tile(i, tm) - ntp), 0))]
    full = lambda a: pl.BlockSpec(a.shape, lambda i: (0, 0))
    consts = [ln_w.reshape(1, dr), ln_b.reshape(1, dr)]
    return pl.pallas_call(
        functools.partial(_rw_post_kernel, ntp),
        grid=(nt // tm,),
        in_specs=specs + [tspec, tspec] + [full(c) for c in consts],
        out_specs=tspec,
        out_shape=jax.ShapeDtypeStruct((nt, dr), BF16),
        compiler_params=_cparams("parallel"),
        name="rw_post",
    )(yf_p, yb_p, y_s, y_s, bonus, g, *consts)


def _rwkv_mixer(geo, p_rw, s0_sample, mu, w0, w2, a0, a2, g2, k_k, k_a, r_k, ln_w, ln_b):
    dr = k_k.shape[0]
    h = dr // RW_HEAD
    outs = _rw_prep(geo, p_rw, mu, w0, w2, a0, a2, g2, k_k, k_a, r_k)
    fwd, bwd, (g, bonus) = outs[0:6], outs[6:12], outs[12:14]
    s0 = s0_sample.transpose(4, 3, 1, 0, 2).reshape(RW_HEAD, RW_HEAD, 2 * geo.bs * h)
    y_s, _ = _rw_scan(*_to_streams(fwd, bwd, geo.np_, geo.bs, geo.ls, None), s0)
    y_s = _from_streams(y_s, dr)
    y_p, s_p = [], []
    for direction in range(2):
        zero = jnp.zeros((RW_HEAD, RW_HEAD, geo.bp * h), F32)
        y_d, s_d = _rw_scan(*_to_streams(fwd, bwd, 0, geo.bp, geo.lp, direction), zero)
        y_p.append(_from_streams(y_d, dr))
        s_p.append(s_d.reshape(RW_HEAD, RW_HEAD, geo.bp, h))
    states = jnp.stack(s_p).transpose(3, 0, 4, 2, 1)
    y_rw = _rw_post(geo, y_p[0].reshape(geo.np_, dr), y_p[1].reshape(geo.np_, dr), y_s.reshape(2 * geo.ns, dr),
                    bonus, g, ln_w, ln_b)
    return y_rw, states


def _gla_dir(q, k, v, gkd, gk2, gkb, s_ref, o_ref, reverse):
    c, dk = q.shape
    hk, hv = dk // GLA_H, v.shape[1] // GLA_H
    z = jnp.dot(gkd, gk2, precision=HI, preferred_element_type=F32) + gkb
    lg = (jnp.minimum(z, 0.0) - jnp.log1p(jnp.exp(-jnp.abs(z)))) * (1.0 / GLA_GATE_NORM)
    row = lax.broadcasted_iota(jnp.int32, (c, c), 0)
    col = lax.broadcasted_iota(jnp.int32, (c, c), 1)
    tri = (col >= row) if reverse else (col <= row)
    b = jnp.dot(tri.astype(F32), lg, precision=HI, preferred_element_type=F32)
    i_ref = c // 2 if reverse else c // 2 - 1
    i_last = 0 if reverse else c - 1
    b_ref = b[i_ref:i_ref + 1, :]
    q_in = (q * jnp.exp(b - b_ref)).astype(BF16)
    k_in = (k * jnp.exp(b_ref - b)).astype(BF16)
    q_dec = (q * jnp.exp(b)).astype(BF16)
    bt = b.T
    bt_last = bt[:, i_last:i_last + 1]
    k_dec_t = (k.T * jnp.exp(bt_last - bt)).astype(BF16)
    chunk_decay = jnp.exp(bt_last)
    vb = v.astype(BF16)
    for h in range(GLA_H):
        ks = slice(h * hk, (h + 1) * hk)
        vs = slice(h * hv, (h + 1) * hv)
        s = s_ref[h]
        scores = lax.dot_general(q_in[:, ks], k_in[:, ks], (((1,), (1,)), ((), ())), preferred_element_type=F32)
        scores = jnp.where(tri, scores, 0.0).astype(BF16)
        o_ref[:, vs] = (jnp.dot(scores, vb[:, vs], preferred_element_type=F32)
                        + jnp.dot(q_dec[:, ks], s.astype(BF16), preferred_element_type=F32))
        s_ref[h] = s * chunk_decay[ks] + jnp.dot(k_dec_t[ks], vb[:, vs], preferred_element_type=F32)


def _gla_kernel(scale, *refs):
    (qf_ref, kf_ref, vf_ref, df_ref, qb_ref, kb_ref, vb_ref, db_ref, gk2_ref, gkb_ref, s0_ref,
     of_ref, ob_ref, so_ref, sf, sb) = refs
    ci = pl.program_id(1)

    @pl.when(ci == 0)
    def _():
        sf[...] = s0_ref[0, 0]
        sb[...] = s0_ref[0, 1]

    rank = gk2_ref.shape[1]
    _gla_dir(qf_ref[...] * scale, kf_ref[...], vf_ref[...], df_ref[:, 0:rank], gk2_ref[0], gkb_ref[0],
             sf, of_ref, False)
    _gla_dir(qb_ref[...] * scale, kb_ref[...], vb_ref[...], db_ref[:, rank:2 * rank], gk2_ref[1], gkb_ref[1],
             sb, ob_ref, True)

    @pl.when(ci == pl.num_programs(1) - 1)
    def _():
        so_ref[0, 0] = sf[...]
        so_ref[0, 1] = sb[...]


def _gla_scan(p, gkd, row0, b, l, dk, dv, gk2, gk_b, s0):
    hk, hv = dk // GLA_H, dv // GLA_H
    c = GLA_CHUNK
    n = l // c
    blk0 = row0 // c
    rank = gk2.shape[1]
    assert (2 * dk) % dv == 0

    def tok(bi, ci, rev):
        return blk0 + bi * n + (n - 1 - ci if rev else ci)

    def specs(rev):
        return [pl.BlockSpec((c, dk), lambda bi, ci: (tok(bi, ci, rev), 0)),
                pl.BlockSpec((c, dk), lambda bi, ci: (tok(bi, ci, rev), 1)),
                pl.BlockSpec((c, dv), lambda bi, ci: (tok(bi, ci, rev), 2 * dk // dv)),
                pl.BlockSpec((c, LANES), lambda bi, ci: (tok(bi, ci, rev), 0))]

    def ospec(rev):
        return pl.BlockSpec((c, dv), lambda bi, ci: (bi * n + (n - 1 - ci if rev else ci), 0))

    sspec = pl.BlockSpec((1, 2, GLA_H, hk, hv), lambda bi, ci: (bi, 0, 0, 0, 0))
    return pl.pallas_call(
        functools.partial(_gla_kernel, float(hk) ** -0.5),
        grid=(b, n),
        in_specs=specs(False) + specs(True) + [
            pl.BlockSpec((2, rank, dk), lambda bi, ci: (0, 0, 0)),
            pl.BlockSpec((2, 1, dk), lambda bi, ci: (0, 0, 0)),
            sspec],
        out_specs=[ospec(False), ospec(True), sspec],
        out_shape=[jax.ShapeDtypeStruct((b * l, dv), F32), jax.ShapeDtypeStruct((b * l, dv), F32),
                   jax.ShapeDtypeStruct((b, 2, GLA_H, hk, hv), F32)],
        scratch_shapes=[pltpu.VMEM((GLA_H, hk, hv), F32), pltpu.VMEM((GLA_H, hk, hv), F32)],
        compiler_params=_cparams("parallel", "arbitrary"),
        name="gla_scan",
    )(p, p, p, gkd, p, p, p, gkd, gk2, gk_b.reshape(2, 1, dk), s0)


def _gla_post_kernel(hv, prompt_tiles, ofp_ref, obp_ref, ofs_ref, obs_ref, g_ref, nw_ref, o_ref):
    is_prompt = pl.program_id(0) < prompt_tiles
    o = jnp.where(is_prompt, ofp_ref[...] + obp_ref[...], ofs_ref[...] + obs_ref[...])
    g = g_ref[...]
    for h in range(GLA_H):
        oh = o[:, h * hv:(h + 1) * hv]
        gh = g[:, h * hv:(h + 1) * hv]
        oh = oh * lax.rsqrt(jnp.mean(oh * oh, axis=-1, keepdims=True) + EPS) * nw_ref[...]
        o_ref[:, h * hv:(h + 1) * hv] = (oh * (gh * jax.nn.sigmoid(gh))).astype(o_ref.dtype)


def _gla_post(geo, of_p, ob_p, of_s, ob_s, p, dk, dv, norm_w):
    nt = p.shape[0]
    tm = _pow2_tile(geo.tm, 256)
    ntp = geo.np_ // tm
    hv = dv // GLA_H
    tspec = pl.BlockSpec((tm, dv), lambda i: (i, 0))
    pspec = pl.BlockSpec((tm, dv), lambda i: (jnp.minimum(i, ntp - 1), 0))
    sspec = pl.BlockSpec((tm, dv), lambda i: (jnp.maximum(i - ntp, 0), 0))
    return pl.pallas_call(
        functools.partial(_gla_post_kernel, hv, ntp),
        grid=(nt // tm,),
        in_specs=[pspec, pspec, sspec, sspec, pl.BlockSpec((tm, dv), lambda i: (i, 2 * dk // dv + 1)),
                  pl.BlockSpec((1, hv), lambda i: (0, 0))],
        out_specs=tspec,
        out_shape=jax.ShapeDtypeStruct((nt, dv), BF16),
        compiler_params=_cparams("parallel"),
        name="gla_post",
    )(of_p, ob_p, of_s, ob_s, p, norm_w.reshape(1, hv))


def _gla_mixer(geo, p, gkd, s0_sample, gk2, gk_b, norm_w, dk, dv):
    hk, hv = dk // GLA_H, dv // GLA_H
    s0_p = jnp.zeros((geo.bp, 2, GLA_H, hk, hv), F32)
    of_p, ob_p, s_p = _gla_scan(p, gkd, 0, geo.bp, geo.lp, dk, dv, gk2, gk_b, s0_p)
    of_s, ob_s, _ = _gla_scan(p, gkd, geo.np_, geo.bs, geo.ls, dk, dv, gk2, gk_b, s0_sample)
    return _gla_post(geo, of_p, ob_p, of_s, ob_s, p, dk, dv, norm_w), s_p


def _conv_gate_kernel(geo, tm, g_ref, gp_ref, gn_ref, v_ref, w_ref, o_ref):
    i = pl.program_id(0)
    is_prompt = i * tm < geo.np_
    halo = GRID_W
    n_ext = tm + 2 * halo

    def conv(lseq, ncol):
        ext = jnp.concatenate([gp_ref[...], g_ref[...], gn_ref[...]], axis=0)
        epos = (lax.broadcasted_iota(jnp.int32, (n_ext, 1), 0) + (i * tm - halo)) & (lseq - 1)
        scol = epos & (ncol - 1)
        taps = {-1: pltpu.roll(jnp.where(scol == ncol - 1, 0.0, ext), 1, axis=0),
                0: ext,
                1: pltpu.roll(jnp.where(scol == 0, 0.0, ext), n_ext - 1, axis=0)}

        def row_sum(di):
            start = halo + di * GRID_W
            return sum(taps[dj][start:start + tm, :] * w_ref[di + 1, dj + 1] for dj in (-1, 0, 1))

        acc = row_sum(0)
        nrow = lseq // ncol
        if nrow > 1:
            assert ncol == GRID_W
            tpos = (lax.broadcasted_iota(jnp.int32, (tm, 1), 0) + i * tm) & (lseq - 1)
            row = tpos >> int(math.log2(ncol))
            acc = acc + jnp.where(row >= 1, row_sum(-1), 0.0) + jnp.where(row <= nrow - 2, row_sum(1), 0.0)
        o_ref[...] = (acc * jax.nn.sigmoid(acc) * v_ref[...]).astype(o_ref.dtype)

    @pl.when(is_prompt)
    def _():
        conv(geo.lp, geo.lp)

    @pl.when(jnp.logical_not(is_prompt))
    def _():
        conv(geo.ls, GRID_W)


def _conv_gate(geo, u, conv_w):
    nt = u.shape[0]
    f = u.shape[1] // 2
    tm = _pow2_tile(geo.tm, 256)
    assert tm % GRID_W == 0
    tc = _col_tile(f, 512)
    ncb = f // tc
    r = tm // GRID_W
    last = nt // GRID_W - 1
    return pl.pallas_call(
        functools.partial(_conv_gate_kernel, geo, tm),
        grid=(nt // tm, ncb),
        in_specs=[pl.BlockSpec((tm, tc), lambda i, j: (i, j)),
                  pl.BlockSpec((GRID_W, tc), lambda i, j: (jnp.maximum(i * r - 1, 0), j)),
                  pl.BlockSpec((GRID_W, tc), lambda i, j: (jnp.minimum((i + 1) * r, last), j)),
                  pl.BlockSpec((tm, tc), lambda i, j: (i, ncb + j)),
                  pl.BlockSpec((3, 3, 1, tc), lambda i, j: (0, 0, 0, j))],
        out_specs=pl.BlockSpec((tm, tc), lambda i, j: (i, j)),
        out_shape=jax.ShapeDtypeStruct((nt, f), BF16),
        compiler_params=_cparams("parallel", "arbitrary"),
        name="conv_gate",
    )(u, u, u, u, conv_w.reshape(3, 3, 1, f))


def _pad_cols(w, mult):
    n = w.shape[-1]
    pad = _round_up(n, mult) - n
    return jnp.pad(w, [(0, 0)] * (w.ndim - 1) + [(0, pad)]) if pad else w


def kernel(x_prompt, x_sample, state_rwkv, state_gla, c, c_ctx, w_ada, b_ada, norm_mix, norm_ffn, ffn_w_up, ffn_conv, ffn_w_down, norm_final, ev_w_in, ev_w_out, hy_short_w, hy_short_b, hy_w1, hy_b1, hy_w2, hy_b2, hy_w3, hy_freq, hy_bias, rw_mu, rw_w0, rw_w2, rw_a0, rw_a2, rw_g2, rw_kk, rw_ka, rw_rk, rw_ln_w, rw_ln_b, od_w_in, od_w_out, gla_gk2, gla_gk_b, gla_norm):
    bp, lp, d = x_prompt.shape
    bs, ls, _ = x_sample.shape
    geo = _Geo(bp, lp, bs, ls, d)
    depth = w_ada.shape[0]
    x = jnp.concatenate([x_prompt.reshape(geo.np_, d), x_sample.reshape(geo.ns, d)], axis=0)

    cond = jnp.concatenate([c_ctx[None, :], c], axis=0)
    cond = jnp.pad(cond, ((0, SUBLANES - cond.shape[0]), (0, 0)))
    mods_all = _adaln(cond, w_ada, b_ada)

    rw_states, gla_states = [], []
    for layer in range(depth):
        mods = mods_all[layer].reshape(SUBLANES * 6, 1, d)
        if layer % 2 == 0:
            e = layer // 2
            dh = hy_bias.shape[1]
            dr = rw_kk.shape[1]
            perm = _head_minor_perm(dr)
            perm3 = np.concatenate([perm, dr + perm, 2 * dr + perm, np.arange(3 * dr, ev_w_in.shape[2] - 3 * dh)])
            w_hy = ev_w_in[e][:, :3 * dh].astype(BF16)
            w_rw = _pad_cols(ev_w_in[e][:, 3 * dh:][:, perm3], 512).astype(BF16)
            p_hy = _norm_mm(geo, x, norm_mix[layer], mods, 0, w_hy)
            p_rw = _norm_mm(geo, x, norm_mix[layer], mods, 0, w_rw)
            x0, u = _hy_prep(geo, p_hy, hy_short_w[e], hy_short_b[e])
            filt_p = _hyena_filter(lp, hy_w1[e], hy_b1[e], hy_w2[e], hy_b2[e], hy_w3[e], hy_freq[e], dh)
            filt_s = _hyena_filter(ls, hy_w1[e], hy_b1[e], hy_w2[e], hy_b2[e], hy_w3[e], hy_freq[e], dh)
            y_hy = _hyena_long_conv(geo, x0, u, filt_p, filt_s, hy_bias[e])
            mu = _pad_cols(rw_mu[e][perm3], 512)
            y_rw, s_ctx = _rwkv_mixer(geo, p_rw, state_rwkv[:, e], mu, rw_w0[e][:, perm], rw_w2[e][:, :, perm],
                                      rw_a0[e][:, perm], rw_a2[e][:, :, perm], rw_g2[e][:, perm], rw_kk[e][perm],
                                      rw_ka[e][perm], rw_rk[e].reshape(dr)[perm], rw_ln_w[e][perm], rw_ln_b[e][perm])
            rw_states.append(s_ctx)
            w_out = jnp.concatenate([ev_w_out[e][:dh], ev_w_out[e][dh:][perm]], axis=0).astype(BF16)
            x = _res_mm(geo, (y_hy, y_rw), w_out, x, mods, 2)
        else:
            o = layer // 2
            dk = gla_gk2.shape[3]
            dv = gla_norm.shape[1] * GLA_H
            w_main = od_w_in[o][:, :2 * dk + 2 * dv].astype(BF16)
            w_gk = _pad_cols(od_w_in[o][:, 2 * dk + 2 * dv:], LANES).astype(BF16)
            p = _norm_mm(geo, x, norm_mix[layer], mods, 0, w_main)
            gkd = _norm_mm(geo, x, norm_mix[layer], mods, 0, w_gk)
            y, s_ctx = _gla_mixer(geo, p, gkd, state_gla[:, o], gla_gk2[o], gla_gk_b[o], gla_norm[o], dk, dv)
            gla_states.append(s_ctx)
            x = _res_mm(geo, y, od_w_out[o].astype(BF16), x, mods, 2)
        u_ffn = _norm_mm(geo, x, norm_ffn[layer], mods, 3, ffn_w_up[layer].astype(BF16))
        act = _conv_gate(geo, u_ffn, ffn_conv[layer])
        x = _res_mm(geo, act, ffn_w_down[layer].astype(BF16), x, mods, 5)

    y = _final_norm(geo, x, norm_final)
    y_prompt = y[:geo.np_].reshape(bp, lp, d)
    y_sample = y[geo.np_:].reshape(bs, ls, d)
    return (y_prompt, y_sample, jnp.stack(rw_states, axis=1), jnp.stack(gla_states, axis=1))
```

```python
import functools
import math

import numpy as np
import jax
import jax.numpy as jnp
from jax import lax
from jax.experimental import pallas as pl
from jax.experimental.pallas import tpu as pltpu

F32 = jnp.float32
BF16 = jnp.bfloat16
HI = lax.Precision.HIGHEST

EPS = 1e-6
RW_HEAD = 64
RW_GN_EPS = 64e-5
GLA_H = 4
GLA_GATE_NORM = 16.0
GLA_CHUNK = 64
GRID_W = 64
HY_TARGET = 1e-2
HY_FAST = 0.3
HY_SLOW = 1.5

LANES = 128
SUBLANES = 8
MXU_DIM = 256
VMEM_LIMIT = 56 * 1024 * 1024
FFT_MINOR = 128


def _cparams(*sem):
    return pltpu.CompilerParams(dimension_semantics=sem, vmem_limit_bytes=VMEM_LIMIT)


def _round_up(n, m):
    return (n + m - 1) // m * m


def _pow2_tile(n, pref):
    t = 1
    while t * 2 <= pref and n % (t * 2) == 0:
        t *= 2
    return t


def _col_tile(n, pref):
    if n % LANES:
        return n
    divisors = [t for t in range(LANES, min(n, pref) + 1, LANES) if n % t == 0]
    mxu_wide = [t for t in divisors if t % MXU_DIM == 0]
    return max(mxu_wide or divisors)


class _Geo:
    def __init__(self, bp, lp, bs, ls, d):
        self.bp, self.lp, self.bs, self.ls, self.d = bp, lp, bs, ls, d
        self.np_, self.ns = bp * lp, bs * ls
        self.nt = self.np_ + self.ns
        self.tm = _pow2_tile(math.gcd(self.np_, ls), 512)
        assert lp & (lp - 1) == 0 and ls & (ls - 1) == 0, "sequence lengths must be powers of two"
        assert self.np_ % ls == 0, "sample rows must start on a sequence-length boundary"
        assert bs + 1 <= SUBLANES

    def row_tile(self, pref):
        return _pow2_tile(math.gcd(self.np_, self.ls), pref)

    def modrow(self, i, tm):
        r0 = i * tm
        return jnp.where(r0 < self.np_, 0, 1 + (r0 - self.np_) // self.ls)

    def seq_len(self, i, tm):
        return jnp.where(i * tm < self.np_, self.lp, self.ls)

    def mirror_tile(self, i, tm):
        def mirrored(tiles_per_seq):
            return (i // tiles_per_seq) * tiles_per_seq + (tiles_per_seq - 1 - i % tiles_per_seq)
        return jnp.where(i * tm < self.np_, mirrored(self.lp // tm), mirrored(self.ls // tm))

    @staticmethod
    def streams_per_group(dr):
        return LANES // (dr // RW_HEAD)

    def stream_slot(self, i, tm, direction, dr):
        r0 = i * tm
        batch = jnp.where(r0 < self.np_, r0 // self.lp, (r0 - self.np_) // self.ls)
        nbatch = jnp.where(r0 < self.np_, self.bp, self.bs)
        return (direction * nbatch + batch) % self.streams_per_group(dr)


def _adaln_kernel(c_ref, w_ref, b_ref, o_ref):
    c = c_ref[...]
    s = c * jax.nn.sigmoid(c)
    o_ref[0] = jnp.dot(s, w_ref[0], precision=HI, preferred_element_type=F32) + b_ref[0]


def _adaln(cond, w_ada, b_ada):
    depth, d, n = w_ada.shape
    tn = _col_tile(n, 1024)
    return pl.pallas_call(
        _adaln_kernel,
        grid=(depth, n // tn),
        in_specs=[pl.BlockSpec((SUBLANES, d), lambda l, j: (0, 0)),
                  pl.BlockSpec((1, d, tn), lambda l, j: (l, 0, j)),
                  pl.BlockSpec((1, 1, tn), lambda l, j: (l, 0, j))],
        out_specs=pl.BlockSpec((1, SUBLANES, tn), lambda l, j: (l, 0, j)),
        out_shape=jax.ShapeDtypeStruct((depth, SUBLANES, n), F32),
        compiler_params=_cparams("parallel", "arbitrary"),
        name="adaln",
    )(cond, w_ada, b_ada.reshape(depth, 1, n))


def _norm_mm_kernel(x_ref, g_ref, sh_ref, sc_ref, w_ref, o_ref, xn_ref):
    @pl.when(pl.program_id(1) == 0)
    def _():
        x = x_ref[...]
        ms = jnp.mean(x * x, axis=-1, keepdims=True)
        y = x * lax.rsqrt(ms + EPS) * g_ref[...]
        xn_ref[...] = (y * (1.0 + sc_ref[0]) + sh_ref[0]).astype(BF16)

    o_ref[...] = jnp.dot(xn_ref[...], w_ref[...], preferred_element_type=F32).astype(o_ref.dtype)


def _norm_mm(geo, x, g, mods, which_shift, w, out_dtype=F32, tm_pref=1024, tn_pref=1024):
    nt, d = x.shape
    n = w.shape[1]
    tm = geo.row_tile(tm_pref)
    tn = _col_tile(n, tn_pref)

    def mod_map(which):
        return lambda i, j: (geo.modrow(i, tm) * 6 + which, 0, 0)

    return pl.pallas_call(
        _norm_mm_kernel,
        grid=(nt // tm, n // tn),
        in_specs=[pl.BlockSpec((tm, d), lambda i, j: (i, 0)),
                  pl.BlockSpec((1, d), lambda i, j: (0, 0)),
                  pl.BlockSpec((1, 1, d), mod_map(which_shift)),
                  pl.BlockSpec((1, 1, d), mod_map(which_shift + 1)),
                  pl.BlockSpec((d, tn), lambda i, j: (0, j))],
        out_specs=pl.BlockSpec((tm, tn), lambda i, j: (i, j)),
        out_shape=jax.ShapeDtypeStruct((nt, n), out_dtype),
        scratch_shapes=[pltpu.VMEM((tm, d), BF16)],
        compiler_params=_cparams("parallel", "arbitrary"),
        name="norm_mm",
    )(x, g.reshape(1, d), mods, mods, w)


def _res_mm_kernel(a_ref, w_ref, res_ref, gt_ref, o_ref):
    acc = jnp.dot(a_ref[...], w_ref[...], preferred_element_type=F32)
    o_ref[...] = res_ref[...] + gt_ref[0] * acc


def _res_mm2_kernel(a1_ref, a2_ref, w_ref, res_ref, gt_ref, o_ref):
    k1 = a1_ref.shape[1]
    acc = jnp.dot(a1_ref[...], w_ref[0:k1, :], preferred_element_type=F32)
    acc = acc + jnp.dot(a2_ref[...], w_ref[k1:, :], preferred_element_type=F32)
    o_ref[...] = res_ref[...] + gt_ref[0] * acc


def _res_mm(geo, a, w, res, mods, which_gate, tm_pref=1024):
    parts = a if isinstance(a, (tuple, list)) else (a,)
    nt = parts[0].shape[0]
    k = sum(p.shape[1] for p in parts)
    n = w.shape[1]
    tm = geo.row_tile(tm_pref)
    tn = _col_tile(n, 1024 if k <= 2048 else 512)
    return pl.pallas_call(
        _res_mm_kernel if len(parts) == 1 else _res_mm2_kernel,
        grid=(nt // tm, n // tn),
        in_specs=[pl.BlockSpec((tm, p.shape[1]), lambda i, j: (i, 0)) for p in parts] + [
                  pl.BlockSpec((k, tn), lambda i, j: (0, j)),
                  pl.BlockSpec((tm, tn), lambda i, j: (i, j)),
                  pl.BlockSpec((1, 1, tn), lambda i, j: (geo.modrow(i, tm) * 6 + which_gate, 0, j))],
        out_specs=pl.BlockSpec((tm, tn), lambda i, j: (i, j)),
        out_shape=jax.ShapeDtypeStruct((nt, n), F32),
        compiler_params=_cparams("parallel", "arbitrary"),
        name="res_mm",
    )(*parts, w, res, mods)


def _final_norm_kernel(x_ref, g_ref, o_ref):
    x = x_ref[...]
    ms = jnp.mean(x * x, axis=-1, keepdims=True)
    o_ref[...] = x * lax.rsqrt(ms + EPS) * g_ref[...]


def _final_norm(geo, x, g):
    nt, d = x.shape
    tm = geo.tm
    return pl.pallas_call(
        _final_norm_kernel,
        grid=(nt // tm,),
        in_specs=[pl.BlockSpec((tm, d), lambda i: (i, 0)), pl.BlockSpec((1, d), lambda i: (0, 0))],
        out_specs=pl.BlockSpec((tm, d), lambda i: (i, 0)),
        out_shape=jax.ShapeDtypeStruct((nt, d), F32),
        compiler_params=_cparams("parallel"),
        name="final_norm",
    )(x, g.reshape(1, d))


def _seq_pos(geo, i, tm):
    lseq = geo.seq_len(i, tm)
    rid = lax.broadcasted_iota(jnp.int32, (tm, 1), 0) + i * tm
    return rid & (lseq - 1), lseq


def _neighbours(x, prev_row, next_row, tpos, lseq):
    tm = x.shape[0]
    rid = lax.broadcasted_iota(jnp.int32, (tm, 1), 0)
    xm = pltpu.roll(x, 1, axis=0)
    xm = jnp.where(rid == 0, prev_row, xm)
    xm = jnp.where(tpos == 0, 0.0, xm)
    xp = pltpu.roll(x, tm - 1, axis=0)
    xp = jnp.where(rid == tm - 1, next_row, xp)
    xp = jnp.where(tpos == lseq - 1, 0.0, xp)
    return xm, xp


def _halo_specs(nt, tm, width, col):
    r = tm // SUBLANES
    last = nt // SUBLANES - 1
    prev = pl.BlockSpec((SUBLANES, width), lambda i, j: (jnp.maximum(i * r - 1, 0), col(j)))
    nxt = pl.BlockSpec((SUBLANES, width), lambda i, j: (jnp.minimum((i + 1) * r, last), col(j)))
    return prev, nxt


def _hy_prep_kernel(geo, tm, *refs):
    (x0_ref, x0p_ref, x0n_ref, x1_ref, x1p_ref, x1n_ref, x2_ref, x2p_ref, x2n_ref,
     w_ref, b_ref, o0_ref, u_ref) = refs
    i = pl.program_id(0)
    tpos, lseq = _seq_pos(geo, i, tm)

    def conv(x_ref, p_ref, n_ref, g):
        x = x_ref[...]
        xm, xp = _neighbours(x, p_ref[SUBLANES - 1:SUBLANES, :], n_ref[0:1, :], tpos, lseq)
        return xm * w_ref[0, g] + x * w_ref[1, g] + xp * w_ref[2, g] + b_ref[g]

    c0 = conv(x0_ref, x0p_ref, x0n_ref, 0)
    c1 = conv(x1_ref, x1p_ref, x1n_ref, 1)
    c2 = conv(x2_ref, x2p_ref, x2n_ref, 2)
    o0_ref[...] = c0
    u_ref[...] = c1 * c2


def _hy_prep(geo, p_hy, short_w, short_b):
    nt = p_hy.shape[0]
    dh = p_hy.shape[1] // 3
    tm = _pow2_tile(geo.tm, 256)
    tc = _col_tile(dh, 512)
    ncb = dh // tc
    specs = []
    for g in range(3):
        col = (lambda j, g=g: g * ncb + j)
        specs.append(pl.BlockSpec((tm, tc), lambda i, j, col=col: (i, col(j))))
        specs.extend(_halo_specs(nt, tm, tc, col))
    specs.append(pl.BlockSpec((3, 3, 1, tc), lambda i, j: (0, 0, 0, j)))
    specs.append(pl.BlockSpec((3, 1, tc), lambda i, j: (0, 0, j)))
    out_spec = pl.BlockSpec((tm, tc), lambda i, j: (i, j))
    w4 = short_w.reshape(3, 3, 1, dh)
    b3 = short_b.reshape(3, 1, dh)
    return pl.pallas_call(
        functools.partial(_hy_prep_kernel, geo, tm),
        grid=(nt // tm, ncb),
        in_specs=specs,
        out_specs=[out_spec, out_spec],
        out_shape=[jax.ShapeDtypeStruct((nt, dh), F32)] * 2,
        compiler_params=_cparams("parallel", "arbitrary"),
        name="hy_prep",
    )(*([p_hy] * 9), w4, b3)


def _lmm_kernel(has_epi, *refs):
    if has_epi:
        a_ref, x_ref, x0_ref, u_ref, bias_ref, o_ref = refs
    else:
        a_ref, x_ref, o_ref = refs
    acc = jnp.dot(a_ref[0], x_ref[0], precision=HI, preferred_element_type=F32)
    if has_epi:
        acc = x0_ref[0] * (acc + u_ref[0] * bias_ref[...])
    o_ref[0] = acc.astype(o_ref.dtype)


def _lmm(a, x, epi=None, out_dtype=F32):
    gm, m, k = a.shape
    g, _, n = x.shape
    tn = _col_tile(n, 2048)
    xspec = pl.BlockSpec((1, k, tn), lambda gi, j: (gi, 0, j))
    ospec = pl.BlockSpec((1, m, tn), lambda gi, j: (gi, 0, j))
    specs = [pl.BlockSpec((1, m, k), lambda gi, j: (gi % gm, 0, 0)), xspec]
    args = [a, x]
    if epi is not None:
        x0, u, bias = epi
        specs += [ospec, ospec, pl.BlockSpec((1, tn), lambda gi, j: (0, j))]
        args += [x0, u, bias]
    return pl.pallas_call(
        functools.partial(_lmm_kernel, epi is not None),
        grid=(g, n // tn),
        in_specs=specs,
        out_specs=ospec,
        out_shape=jax.ShapeDtypeStruct((g, m, n), out_dtype),
        compiler_params=_cparams("parallel", "arbitrary"),
        name="dft_lmm",
    )(*args)


def _spec_kernel(has_epi, *refs):
    if has_epi:
        mf_ref, h_ref, mi_ref, x_ref, x0_ref, bias_ref, o_ref = refs
    else:
        mf_ref, h_ref, mi_ref, x_ref, o_ref = refs
    x = x_ref[0]
    f = jnp.dot(mf_ref[0], x, precision=HI, preferred_element_type=F32)
    r = f.shape[0] // 2
    fr, fi = f[:r], f[r:]
    hr, hi = h_ref[0, :r], h_ref[0, r:]
    y = jnp.concatenate([fr * hr - fi * hi, fr * hi + fi * hr], axis=0)
    out = jnp.dot(mi_ref[0], y, precision=HI, preferred_element_type=F32)
    if has_epi:
        out = x0_ref[0] * (out + x * bias_ref[...])
    o_ref[0] = out.astype(o_ref.dtype)


def _spectral(mf, h, mi, x, epi=None, out_dtype=F32):
    gm, r2, k = mf.shape
    kout = mi.shape[1]
    g, _, c = x.shape
    tc = _col_tile(c, 512)
    xspec = pl.BlockSpec((1, k, tc), lambda gi, j: (gi, 0, j))
    ospec = pl.BlockSpec((1, kout, tc), lambda gi, j: (gi, 0, j))
    specs = [pl.BlockSpec((1, r2, k), lambda gi, j: (gi % gm, 0, 0)),
             pl.BlockSpec((1, r2, tc), lambda gi, j: (gi % gm, 0, j)),
             pl.BlockSpec((1, kout, r2), lambda gi, j: (gi % gm, 0, 0)),
             xspec]
    args = [mf, h, mi, x]
    if epi is not None:
        x0, bias = epi
        specs += [ospec, pl.BlockSpec((1, tc), lambda gi, j: (0, j))]
        args += [x0, bias]
    return pl.pallas_call(
        functools.partial(_spec_kernel, epi is not None),
        grid=(g, c // tc),
        in_specs=specs,
        out_specs=ospec,
        out_shape=jax.ShapeDtypeStruct((g, kout, c), out_dtype),
        compiler_params=_cparams("parallel", "arbitrary"),
        name="dft_spectral",
    )(*args)


def _dft_tables_direct(l):
    n = 2 * l
    nf = l + 1
    r = _round_up(nf, SUBLANES)
    kk = np.arange(nf)[:, None].astype(np.float64)
    ang_half = 2.0 * np.pi * ((kk * np.arange(l)[None, :]) % n) / n
    ang_full = 2.0 * np.pi * ((kk * np.arange(n)[None, :]) % n) / n

    def fwd(ang):
        m = np.zeros((2 * r, ang.shape[1]))
        m[:nf] = np.cos(ang)
        m[r:r + nf] = -np.sin(ang)
        return m

    coef = np.full((nf,), 2.0 / n)
    coef[0] = coef[-1] = 1.0 / n
    inv = np.zeros((l, 2 * r))
    inv[:, :nf] = (np.cos(ang_half) * coef[:, None]).T
    inv[:, r:r + nf] = (-np.sin(ang_half) * coef[:, None]).T
    f32 = lambda a: jnp.asarray(a[None], F32)
    return f32(fwd(ang_half)), f32(fwd(ang_full)), f32(inv)


def _dft_tables_two_level(l):
    n = 2 * l
    n1 = FFT_MINOR
    n2 = n // n1
    nb = n2 // 2 + 1
    nbp = _round_up(nb, 4)
    k2 = np.arange(nb)[:, None].astype(np.float64)
    ang2 = 2.0 * np.pi * ((k2 * np.arange(n2)[None, :]) % n2) / n2
    ms_full = np.zeros((2 * nbp, n2))
    ms_full[0:2 * nb:2] = np.cos(ang2)
    ms_full[1:2 * nb:2] = -np.sin(ang2)
    ms_half = ms_full[:, :n2 // 2]
    coef = np.full((nb,), 2.0 / n)
    coef[0] = coef[-1] = 1.0 / n
    ms_inv = np.zeros((n2 // 2, 2 * nbp))
    ms_inv[:, 0:2 * nb:2] = (np.cos(ang2[:, :n2 // 2]) * coef[:, None]).T
    ms_inv[:, 1:2 * nb:2] = (-np.sin(ang2[:, :n2 // 2]) * coef[:, None]).T

    a_fwd = np.zeros((nbp, 2 * n1, 2 * n1))
    a_inv = np.zeros((nbp, 2 * n1, 2 * n1))
    i1 = np.arange(n1).astype(np.float64)
    for b in range(nb):
        ph = 2.0 * np.pi * (((np.outer(i1, i1) * n2) + (i1[None, :] * b)) % n) / n
        wr, wi = np.cos(ph), -np.sin(ph)
        a_fwd[b] = np.block([[wr, -wi], [wi, wr]])
        ph = 2.0 * np.pi * (((np.outer(i1, i1) * n2) + (i1[:, None] * b)) % n) / n
        pr, pi_ = np.cos(ph), np.sin(ph)
        a_inv[b] = np.block([[pr, -pi_], [pi_, pr]])
    f32 = lambda a: jnp.asarray(a, F32)
    return (f32(ms_half[None]), f32(ms_full[None]), f32(ms_inv[None]), f32(a_fwd), f32(a_inv), n1, n2, nbp)


def _hyena_filter(l, w1, b1, w2, b2, w3, freq, dh):
    rows = jnp.arange(2 * l)[:, None]
    pos = jnp.where(rows < l, rows, 2 * l - rows).astype(F32)
    t = pos / max(l - 1, 1)
    emb = w1.shape[0]
    bands = (emb - 1) // 2
    fb = jnp.linspace(1e-4, bands - 1, bands, dtype=F32)
    ang = (2.0 * math.pi / l) * pos * fb
    z = jnp.concatenate([t, jnp.cos(ang), -jnp.sin(ang)], axis=-1)
    h = jnp.sin(freq[0] * (jnp.dot(z, w1, precision=HI) + b1))
    h = jnp.sin(freq[1] * (jnp.dot(h, w2, precision=HI) + b2))
    h = jnp.dot(h, w3, precision=HI)
    deltas = jnp.abs(jnp.linspace(math.log(HY_TARGET) / HY_SLOW, math.log(HY_TARGET) / HY_FAST, dh, dtype=F32))
    half = jnp.where(rows < l, h[:, :dh], h[:, dh:])
    filt = jnp.where(rows == l, 0.0, half * jnp.exp(-t * deltas))
    return filt / jnp.sum(jnp.abs(filt), axis=0, keepdims=True)


def _hyena_long_conv(geo, x0, u, filt_p, filt_s, bias):
    dh = u.shape[1]
    bias2 = bias.reshape(1, dh)
    mf_half, mf_full, mi = _dft_tables_direct(geo.lp)
    h_p = _lmm(mf_full, filt_p[None])
    up = u[:geo.np_].reshape(geo.bp, geo.lp, dh)
    x0p = x0[:geo.np_].reshape(geo.bp, geo.lp, dh)
    y_p = _spectral(mf_half, h_p, mi, up, epi=(x0p, bias2), out_dtype=BF16)
    ms_half, ms_full, ms_inv, a_fwd, a_inv, n1, n2, nbp = _dft_tables_two_level(geo.ls)
    hs = _lmm(ms_full, filt_s.reshape(1, n2, n1 * dh))
    hs = _lmm(a_fwd, hs.reshape(nbp, 2 * n1, dh))
    us = u[geo.np_:].reshape(geo.bs, n2 // 2, n1 * dh)
    x0s = x0[geo.np_:].reshape(geo.bs, n2 // 2, n1 * dh)
    xs = _lmm(ms_half, us)
    zs = _spectral(a_fwd, hs, a_inv, xs.reshape(geo.bs * nbp, 2 * n1, dh))
    zs = zs.reshape(geo.bs, 2 * nbp, n1 * dh)
    y_s = _lmm(ms_inv, zs, epi=(x0s, us, jnp.tile(bias2, (1, n1))), out_dtype=BF16)
    return jnp.concatenate([y_p.reshape(geo.np_, dh), y_s.reshape(geo.ns, dh)], axis=0)


def _head_sum(x):
    dr = x.shape[1]
    nheads = dr // RW_HEAD
    assert LANES % nheads == 0
    t = x[:, 0:LANES]
    for q in range(1, dr // LANES):
        t = t + x[:, q * LANES:(q + 1) * LANES]
    shift = LANES // 2
    while shift >= nheads:
        t = t + pltpu.roll(t, shift, axis=1)
        shift //= 2
    return jnp.concatenate([t] * (dr // LANES), axis=1)


def _flip_rows(x):
    tm, c = x.shape
    x3 = x.reshape(tm // SUBLANES, SUBLANES, c)
    rid = lax.broadcasted_iota(jnp.int32, (1, SUBLANES, 1), 1)
    for s in (4, 2, 1):
        x3 = jnp.where((rid & s) != 0, pltpu.roll(x3, s, axis=1), pltpu.roll(x3, SUBLANES - s, axis=1))
    return jnp.concatenate([x3[j] for j in reversed(range(tm // SUBLANES))], axis=0)


def _rw_prep_kernel(geo, tm, dr, *refs):
    (x_ref, xp_ref, xn_ref, mu_ref, w0_ref, w2_ref, a0_ref, a2_ref, g2_ref, kk_ref, ka_ref, rk_ref,
     rf_o, nkkf_o, vf_o, wf_o, kxf_o, bf_o, rb_o, nkkb_o, vb_o, wb_o, kxb_o, bb_o, g_o, bon_o) = refs
    i = pl.program_id(0)
    tpos, lseq = _seq_pos(geo, i, tm)
    x = x_ref[...]
    xm, xp = _neighbours(x, xp_ref[SUBLANES - 1:SUBLANES, :], xn_ref[0:1, :], tpos, lseq)
    x = x + mu_ref[...] * (0.5 * (xm + xp) - x)
    r = x[:, 0:dr]
    k = x[:, dr:2 * dr]
    v = x[:, 2 * dr:3 * dr]
    wd = x[:, 3 * dr:3 * dr + LANES]
    ad = x[:, 3 * dr + LANES:3 * dr + 2 * LANES]
    gd = x[:, 3 * dr + 2 * LANES:3 * dr + 3 * LANES]
    wl = w0_ref[...] + jnp.dot(jnp.tanh(wd), w2_ref[...], precision=HI, preferred_element_type=F32)
    w_log = -jax.nn.softplus(-wl) - 0.5
    decay = jnp.exp(-jnp.exp(w_log))
    a = jax.nn.sigmoid(a0_ref[...] + jnp.dot(ad, a2_ref[...], precision=HI, preferred_element_type=F32))
    g_o[...] = jnp.dot(jax.nn.sigmoid(gd), g2_ref[...], precision=HI, preferred_element_type=F32)
    kk = k * kk_ref[...]
    kk = kk * lax.rsqrt(_head_sum(kk * kk) + 1e-12)
    a_0, a_1 = a[:, :dr], a[:, dr:]
    kx0 = k * (1.0 + (a_0 - 1.0) * ka_ref[...])
    kx1 = k * (1.0 + (a_1 - 1.0) * ka_ref[...])
    bon_o[...] = _head_sum(r * (kx0 + kx1) * rk_ref[...]) * v

    slab = LANES // geo.streams_per_group(dr)

    def emit(o_ref, val, direction):
        if direction:
            val = _flip_rows(val)
        shift = geo.stream_slot(i, tm, direction, dr) * slab
        for q in range(dr // LANES):
            o_ref[:, q * LANES:(q + 1) * LANES] = pltpu.roll(val[:, q * LANES:(q + 1) * LANES], shift, axis=1)

    nkk = -kk
    for o_f, o_b, val_f, val_b in ((rf_o, rb_o, r, r), (nkkf_o, nkkb_o, nkk, nkk), (vf_o, vb_o, v, v),
                                   (wf_o, wb_o, decay[:, :dr], decay[:, dr:]), (kxf_o, kxb_o, kx0, kx1),
                                   (bf_o, bb_o, kk * a_0, kk * a_1)):
        emit(o_f, val_f, 0)
        emit(o_b, val_b, 1)


def _blockdiag2(w):
    _, r, c = w.shape
    z = jnp.zeros((r, c), w.dtype)
    return jnp.concatenate([jnp.concatenate([w[0], z], axis=1), jnp.concatenate([z, w[1]], axis=1)], axis=0)


def _head_minor_perm(dr):
    return np.arange(dr).reshape(dr // RW_HEAD, RW_HEAD).T.reshape(-1)


def _rw_prep(geo, p_rw, mu, w0, w2, a0, a2, g2, k_k, k_a, r_k):
    nt, width = p_rw.shape
    dr = k_k.shape[0]
    assert 2 * w2.shape[1] == LANES and 2 * a2.shape[1] == LANES and g2.shape[0] == LANES
    tm = _pow2_tile(min(geo.tm, geo.lp), 128)
    full = lambda shape: pl.BlockSpec(shape, lambda i, j: tuple(0 for _ in shape))
    prev, nxt = _halo_specs(nt, tm, width, lambda j: 0)
    consts = [mu.reshape(1, width), w0.reshape(1, 2 * dr), _blockdiag2(w2), a0.reshape(1, 2 * dr),
              _blockdiag2(a2), g2, k_k.reshape(1, dr), k_a.reshape(1, dr), r_k.reshape(1, dr)]
    ospec = pl.BlockSpec((tm, dr), lambda i, j: (i, 0))
    mspec = pl.BlockSpec((tm, dr), lambda i, j: (geo.mirror_tile(i, tm), 0))
    return pl.pallas_call(
        functools.partial(_rw_prep_kernel, geo, tm, dr),
        grid=(nt // tm, 1),
        in_specs=[pl.BlockSpec((tm, width), lambda i, j: (i, 0)), prev, nxt] + [full(c.shape) for c in consts],
        out_specs=[ospec] * 6 + [mspec] * 6 + [ospec] * 2,
        out_shape=[jax.ShapeDtypeStruct((nt, dr), F32)] * 14,
        compiler_params=_cparams("parallel", "arbitrary"),
        name="rw_prep",
    )(p_rw, p_rw, p_rw, *consts)


def _rw_scan_kernel(tc, r_ref, nkk_ref, v_ref, w_ref, kx_ref, b_ref, s0_ref, y_ref, so_ref, s_ref):
    t_chunk = pl.program_id(1)

    @pl.when(t_chunk == 0)
    def _():
        s_ref[...] = s0_ref[...]

    nacc = 2
    vhalf = RW_HEAD // 2

    def row(ref, t, k):
        return ref[t // SUBLANES, k, pl.ds(t % SUBLANES, 1), :]

    def state_dot_neg_kk(t):
        acc = [jnp.zeros((RW_HEAD, LANES), F32) for _ in range(nacc)]
        for k in range(RW_HEAD):
            acc[k % nacc] = acc[k % nacc] + s_ref[k] * row(nkk_ref, t, k)
        return acc[0] + acc[1]

    def step(t, sa):
        t_next = jnp.minimum(t + 1, tc - 1)
        sa_next = []
        for half in range(2):
            vs = slice(half * vhalf, (half + 1) * vhalf)
            sa_h = sa[vs]
            vv = v_ref[t, vs, :]
            acc_y = [jnp.zeros((vhalf, LANES), F32) for _ in range(nacc)]
            acc_s = [jnp.zeros((vhalf, LANES), F32) for _ in range(nacc)]
            for k in range(RW_HEAD):
                new = s_ref[k, vs, :] * row(w_ref, t, k) + sa_h * row(b_ref, t, k) + vv * row(kx_ref, t, k)
                s_ref[k, vs, :] = new
                acc_y[k % nacc] = acc_y[k % nacc] + new * row(r_ref, t, k)
                acc_s[k % nacc] = acc_s[k % nacc] + new * row(nkk_ref, t_next, k)
            y_ref[t, vs, :] = acc_y[0] + acc_y[1]
            sa_next.append(acc_s[0] + acc_s[1])
        return jnp.concatenate(sa_next, axis=0)

    lax.fori_loop(0, tc, step, state_dot_neg_kk(0))

    @pl.when(t_chunk == pl.num_programs(1) - 1)
    def _():
        so_ref[...] = s_ref[...]


def _rw_scan(r, nkk, v, w, kx, b, s0):
    t_len, n, lanes = v.shape
    tc = _pow2_tile(t_len, 32)
    xspec = pl.BlockSpec((tc, n, LANES), lambda g, t: (t, 0, g))
    rspec = pl.BlockSpec((tc // SUBLANES, n, SUBLANES, LANES), lambda g, t: (t, 0, 0, g))
    sspec = pl.BlockSpec((n, n, LANES), lambda g, t: (0, 0, g))
    return pl.pallas_call(
        functools.partial(_rw_scan_kernel, tc),
        grid=(lanes // LANES, t_len // tc),
        in_specs=[rspec, rspec, xspec, rspec, rspec, rspec, sspec],
        out_specs=[xspec, sspec],
        out_shape=[jax.ShapeDtypeStruct((t_len, n, lanes), F32), jax.ShapeDtypeStruct((n, n, lanes), F32)],
        scratch_shapes=[pltpu.VMEM((n, n, LANES), F32)],
        compiler_params=_cparams("parallel", "arbitrary"),
        name="rw_scan",
    )(r, nkk, v, w, kx, b, s0)


def _merge_slabs(pieces, slab, offset):
    n = len(pieces)
    lane_slab = lax.broadcasted_iota(jnp.int32, pieces[0].shape, 1) // slab
    acc = pieces[n - 1]
    for p in range(n - 1):
        acc = jnp.where(lane_slab == (offset + p) % n, pieces[p], acc)
    return acc


def _to_streams_kernel(nsrc, ntens, slab, value_index, *refs):
    srcs, outs = refs[:ntens * nsrc], refs[ntens * nsrc:]
    for ti in range(ntens):
        tiles = [srcs[ti * nsrc + s][...] for s in range(nsrc)]
        tt = tiles[0].shape[0]
        for j in range(SUBLANES):
            merged = _merge_slabs(tiles, slab, j)
            shift = ((nsrc - j) % nsrc) * slab
            merged = pltpu.roll(merged, shift, axis=1) if shift else merged
            if ti == value_index:
                outs[ti][:, j, :] = merged
            else:
                outs[ti][:, j] = merged.reshape(tt // SUBLANES, SUBLANES, LANES)


def _to_streams(fwd, bwd, row0, nbatch, l, direction):
    ntens = len(fwd)
    dr = fwd[0].shape[1]
    nsrc = _Geo.streams_per_group(dr)
    assert nsrc == SUBLANES, "the stream layout assumes 16 heads (eight channels per 128-lane tile)"
    slab = LANES // nsrc
    tt = _pow2_tile(l, 128)
    if direction is None:
        assert 2 * nbatch == nsrc
        ngroups = 1
        source = lambda g, s: (s // nbatch, s % nbatch)
    else:
        assert nbatch % nsrc == 0
        ngroups = nbatch // nsrc
        source = lambda g, s: (direction, g * nsrc + s)
    specs, args = [], []
    for ti in range(ntens):
        for s in range(nsrc):
            d = source(0, s)[0]
            specs.append(pl.BlockSpec(
                (tt, LANES), lambda g, i, q, s=s: ((row0 + source(g, s)[1] * l) // tt + i, q)))
            args.append((bwd if d else fwd)[ti])
    value_index = 2
    vspec = pl.BlockSpec((tt, SUBLANES, LANES), lambda g, i, q: (i, q, g))
    rspec = pl.BlockSpec((tt // SUBLANES, SUBLANES, SUBLANES, LANES), lambda g, i, q: (i, q, 0, g))
    vshape = jax.ShapeDtypeStruct((l, RW_HEAD, ngroups * LANES), F32)
    rshape = jax.ShapeDtypeStruct((l // SUBLANES, RW_HEAD, SUBLANES, ngroups * LANES), F32)
    return pl.pallas_call(
        functools.partial(_to_streams_kernel, nsrc, ntens, slab, value_index),
        grid=(ngroups, l // tt, dr // LANES),
        in_specs=specs,
        out_specs=[vspec if ti == value_index else rspec for ti in range(ntens)],
        out_shape=[vshape if ti == value_index else rshape for ti in range(ntens)],
        compiler_params=_cparams("parallel", "parallel", "arbitrary"),
        name="to_streams",
    )(*args)


def _from_streams_kernel(nsrc, slab, y_ref, o_ref):
    rolled = [y_ref[:, j, :] if j == 0 else pltpu.roll(y_ref[:, j, :], j * slab, axis=1) for j in range(SUBLANES)]
    for s in range(nsrc):
        merged = _merge_slabs(rolled, slab, s)
        shift = ((nsrc - s) % nsrc) * slab
        o_ref[s] = pltpu.roll(merged, shift, axis=1) if shift else merged


def _from_streams(y, dr):
    l, _, lanes = y.shape
    nsrc = _Geo.streams_per_group(dr)
    slab = LANES // nsrc
    tt = _pow2_tile(l, 128)
    ngroups = lanes // LANES
    return pl.pallas_call(
        functools.partial(_from_streams_kernel, nsrc, slab),
        grid=(ngroups, l // tt, dr // LANES),
        in_specs=[pl.BlockSpec((tt, SUBLANES, LANES), lambda g, i, q: (i, q, g))],
        out_specs=pl.BlockSpec((nsrc, tt, LANES), lambda g, i, q: (g, i, q)),
        out_shape=jax.ShapeDtypeStruct((ngroups * nsrc, l, dr), F32),
        compiler_params=_cparams("parallel", "parallel", "arbitrary"),
        name="from_streams",
    )(y)


def _rw_post_kernel(prompt_tiles, yfp_ref, ybp_ref, yfs_ref, ybs_ref, bon_ref, g_ref, lnw_ref, lnb_ref, o_ref):
    is_prompt = pl.program_id(0) < prompt_tiles
    yf = jnp.where(is_prompt, yfp_ref[...], yfs_ref[...])
    yb = jnp.where(is_prompt, ybp_ref[...], ybs_ref[...])
    y = yf + _flip_rows(yb)
    inv_n = 1.0 / RW_HEAD
    mean = _head_sum(y) * inv_n
    yc = y - mean
    var = _head_sum(yc * yc) * inv_n
    yn = yc * lax.rsqrt(var + RW_GN_EPS) * lnw_ref[...] + lnb_ref[...]
    o_ref[...] = ((yn + bon_ref[...]) * g_ref[...]).astype(o_ref.dtype)


def _rw_post(geo, yf_p, yb_p, y_s, bonus, g, ln_w, ln_b):
    nt, dr = bonus.shape
    tm = _pow2_tile(min(geo.tm, geo.lp), 256)
    ntp, nts = geo.np_ // tm, geo.ns // tm
    tspec = pl.BlockSpec((tm, dr), lambda i: (i, 0))
    specs = [pl.BlockSpec((tm, dr), lambda i: (jnp.minimum(i, ntp - 1), 0)),
             pl.BlockSpec((tm, dr), lambda i: (jnp.where(i < ntp, geo.mirror_tile(i, tm), 0), 0)),
             pl.BlockSpec((tm, dr), lambda i: (jnp.maximum(i - ntp, 0), 0)),
             pl.BlockSpec((tm, dr), lambda i: (nts + jnp.where(i < ntp, 0, geo.mirror_tile(i, tm) - ntp), 0))]
    full = lambda a: pl.BlockSpec(a.shape, lambda i: (0, 0))
    consts = [ln_w.reshape(1, dr), ln_b.reshape(1, dr)]
    return pl.pallas_call(
        functools.partial(_rw_post_kernel, ntp),
        grid=(nt // tm,),
        in_specs=specs + [tspec, tspec] + [full(c) for c in consts],
        out_specs=tspec,
        out_shape=jax.ShapeDtypeStruct((nt, dr), BF16),
        compiler_params=_cparams("parallel"),
        name="rw_post",
    )(yf_p, yb_p, y_s, y_s, bonus, g, *consts)


def _rwkv_mixer(geo, p_rw, s0_sample, mu, w0, w2, a0, a2, g2, k_k, k_a, r_k, ln_w, ln_b):
    dr = k_k.shape[0]
    h = dr // RW_HEAD
    outs = _rw_prep(geo, p_rw, mu, w0, w2, a0, a2, g2, k_k, k_a, r_k)
    fwd, bwd, (g, bonus) = outs[0:6], outs[6:12], outs[12:14]
    s0 = s0_sample.transpose(4, 3, 1, 0, 2).reshape(RW_HEAD, RW_HEAD, 2 * geo.bs * h)
    y_s, _ = _rw_scan(*_to_streams(fwd, bwd, geo.np_, geo.bs, geo.ls, None), s0)
    y_s = _from_streams(y_s, dr)
    y_p, s_p = [], []
    for direction in range(2):
        zero = jnp.zeros((RW_HEAD, RW_HEAD, geo.bp * h), F32)
        y_d, s_d = _rw_scan(*_to_streams(fwd, bwd, 0, geo.bp, geo.lp, direction), zero)
        y_p.append(_from_streams(y_d, dr))
        s_p.append(s_d.reshape(RW_HEAD, RW_HEAD, geo.bp, h))
    states = jnp.stack(s_p).transpose(3, 0, 4, 2, 1)
    y_rw = _rw_post(geo, y_p[0].reshape(geo.np_, dr), y_p[1].reshape(geo.np_, dr), y_s.reshape(2 * geo.ns, dr),
                    bonus, g, ln_w, ln_b)
    return y_rw, states


def _gla_dir(q, k, v, gkd, gk2, gkb, s_ref, o_ref, reverse):
    c, dk = q.shape
    hk, hv = dk // GLA_H, v.shape[1] // GLA_H
    z = jnp.dot(gkd, gk2, precision=HI, preferred_element_type=F32) + gkb
    lg = (jnp.minimum(z, 0.0) - jnp.log1p(jnp.exp(-jnp.abs(z)))) * (1.0 / GLA_GATE_NORM)
    row = lax.broadcasted_iota(jnp.int32, (c, c), 0)
    col = lax.broadcasted_iota(jnp.int32, (c, c), 1)
    tri = (col >= row) if reverse else (col <= row)
    b = jnp.dot(tri.astype(F32), lg, precision=HI, preferred_element_type=F32)
    i_ref = c // 2 if reverse else c // 2 - 1
    i_last = 0 if reverse else c - 1
    b_ref = b[i_ref:i_ref + 1, :]
    q_in = (q * jnp.exp(b - b_ref)).astype(BF16)
    k_in = (k * jnp.exp(b_ref - b)).astype(BF16)
    q_dec = (q * jnp.exp(b)).astype(BF16)
    bt = b.T
    bt_last = bt[:, i_last:i_last + 1]
    k_dec_t = (k.T * jnp.exp(bt_last - bt)).astype(BF16)
    chunk_decay = jnp.exp(bt_last)
    vb = v.astype(BF16)
    for h in range(GLA_H):
        ks = slice(h * hk, (h + 1) * hk)
        vs = slice(h * hv, (h + 1) * hv)
        s = s_ref[h]
        scores = lax.dot_general(q_in[:, ks], k_in[:, ks], (((1,), (1,)), ((), ())), preferred_element_type=F32)
        scores = jnp.where(tri, scores, 0.0).astype(BF16)
        o_ref[:, vs] = (jnp.dot(scores, vb[:, vs], preferred_element_type=F32)
                        + jnp.dot(q_dec[:, ks], s.astype(BF16), preferred_element_type=F32))
        s_ref[h] = s * chunk_decay[ks] + jnp.dot(k_dec_t[ks], vb[:, vs], preferred_element_type=F32)


def _gla_kernel(scale, *refs):
    (qf_ref, kf_ref, vf_ref, df_ref, qb_ref, kb_ref, vb_ref, db_ref, gk2_ref, gkb_ref, s0_ref,
     of_ref, ob_ref, so_ref, sf, sb) = refs
    ci = pl.program_id(1)

    @pl.when(ci == 0)
    def _():
        sf[...] = s0_ref[0, 0]
        sb[...] = s0_ref[0, 1]

    rank = gk2_ref.shape[1]
    _gla_dir(qf_ref[...] * scale, kf_ref[...], vf_ref[...], df_ref[:, 0:rank], gk2_ref[0], gkb_ref[0],
             sf, of_ref, False)
    _gla_dir(qb_ref[...] * scale, kb_ref[...], vb_ref[...], db_ref[:, rank:2 * rank], gk2_ref[1], gkb_ref[1],
             sb, ob_ref, True)

    @pl.when(ci == pl.num_programs(1) - 1)
    def _():
        so_ref[0, 0] = sf[...]
        so_ref[0, 1] = sb[...]


def _gla_scan(p, gkd, row0, b, l, dk, dv, gk2, gk_b, s0):
    hk, hv = dk // GLA_H, dv // GLA_H
    c = GLA_CHUNK
    n = l // c
    blk0 = row0 // c
    rank = gk2.shape[1]
    assert (2 * dk) % dv == 0

    def tok(bi, ci, rev):
        return blk0 + bi * n + (n - 1 - ci if rev else ci)

    def specs(rev):
        return [pl.BlockSpec((c, dk), lambda bi, ci: (tok(bi, ci, rev), 0)),
                pl.BlockSpec((c, dk), lambda bi, ci: (tok(bi, ci, rev), 1)),
                pl.BlockSpec((c, dv), lambda bi, ci: (tok(bi, ci, rev), 2 * dk // dv)),
                pl.BlockSpec((c, LANES), lambda bi, ci: (tok(bi, ci, rev), 0))]

    def ospec(rev):
        return pl.BlockSpec((c, dv), lambda bi, ci: (bi * n + (n - 1 - ci if rev else ci), 0))

    sspec = pl.BlockSpec((1, 2, GLA_H, hk, hv), lambda bi, ci: (bi, 0, 0, 0, 0))
    return pl.pallas_call(
        functools.partial(_gla_kernel, float(hk) ** -0.5),
        grid=(b, n),
        in_specs=specs(False) + specs(True) + [
            pl.BlockSpec((2, rank, dk), lambda bi, ci: (0, 0, 0)),
            pl.BlockSpec((2, 1, dk), lambda bi, ci: (0, 0, 0)),
            sspec],
        out_specs=[ospec(False), ospec(True), sspec],
        out_shape=[jax.ShapeDtypeStruct((b * l, dv), F32), jax.ShapeDtypeStruct((b * l, dv), F32),
                   jax.ShapeDtypeStruct((b, 2, GLA_H, hk, hv), F32)],
        scratch_shapes=[pltpu.VMEM((GLA_H, hk, hv), F32), pltpu.VMEM((GLA_H, hk, hv), F32)],
        compiler_params=_cparams("parallel", "arbitrary"),
        name="gla_scan",
    )(p, p, p, gkd, p, p, p, gkd, gk2, gk_b.reshape(2, 1, dk), s0)


def _gla_post_kernel(hv, prompt_tiles, ofp_ref, obp_ref, ofs_ref, obs_ref, g_ref, nw_ref, o_ref):
    is_prompt = pl.program_id(0) < prompt_tiles
    o = jnp.where(is_prompt, ofp_ref[...] + obp_ref[...], ofs_ref[...] + obs_ref[...])
    g = g_ref[...]
    for h in range(GLA_H):
        oh = o[:, h * hv:(h + 1) * hv]
        gh = g[:, h * hv:(h + 1) * hv]
        oh = oh * lax.rsqrt(jnp.mean(oh * oh, axis=-1, keepdims=True) + EPS) * nw_ref[...]
        o_ref[:, h * hv:(h + 1) * hv] = (oh * (gh * jax.nn.sigmoid(gh))).astype(o_ref.dtype)


def _gla_post(geo, of_p, ob_p, of_s, ob_s, p, dk, dv, norm_w):
    nt = p.shape[0]
    tm = _pow2_tile(geo.tm, 256)
    ntp = geo.np_ // tm
    hv = dv // GLA_H
    tspec = pl.BlockSpec((tm, dv), lambda i: (i, 0))
    pspec = pl.BlockSpec((tm, dv), lambda i: (jnp.minimum(i, ntp - 1), 0))
    sspec = pl.BlockSpec((tm, dv), lambda i: (jnp.maximum(i - ntp, 0), 0))
    return pl.pallas_call(
        functools.partial(_gla_post_kernel, hv, ntp),
        grid=(nt // tm,),
        in_specs=[pspec, pspec, sspec, sspec, pl.BlockSpec((tm, dv), lambda i: (i, 2 * dk // dv + 1)),
                  pl.BlockSpec((1, hv), lambda i: (0, 0))],
        out_specs=tspec,
        out_shape=jax.ShapeDtypeStruct((nt, dv), BF16),
        compiler_params=_cparams("parallel"),
        name="gla_post",
    )(of_p, ob_p, of_s, ob_s, p, norm_w.reshape(1, hv))


def _gla_mixer(geo, p, gkd, s0_sample, gk2, gk_b, norm_w, dk, dv):
    hk, hv = dk // GLA_H, dv // GLA_H
    s0_p = jnp.zeros((geo.bp, 2, GLA_H, hk, hv), F32)
    of_p, ob_p, s_p = _gla_scan(p, gkd, 0, geo.bp, geo.lp, dk, dv, gk2, gk_b, s0_p)
    of_s, ob_s, _ = _gla_scan(p, gkd, geo.np_, geo.bs, geo.ls, dk, dv, gk2, gk_b, s0_sample)
    return _gla_post(geo, of_p, ob_p, of_s, ob_s, p, dk, dv, norm_w), s_p


def _conv_gate_kernel(geo, tm, g_ref, gp_ref, gn_ref, v_ref, w_ref, o_ref):
    i = pl.program_id(0)
    is_prompt = i * tm < geo.np_
    halo = GRID_W
    n_ext = tm + 2 * halo

    def conv(lseq, ncol):
        ext = jnp.concatenate([gp_ref[...].astype(F32), g_ref[...].astype(F32), gn_ref[...].astype(F32)],
                              axis=0)
        epos = (lax.broadcasted_iota(jnp.int32, (n_ext, 1), 0) + (i * tm - halo)) & (lseq - 1)
        scol = epos & (ncol - 1)
        taps = {-1: pltpu.roll(jnp.where(scol == ncol - 1, 0.0, ext), 1, axis=0),
                0: ext,
                1: pltpu.roll(jnp.where(scol == 0, 0.0, ext), n_ext - 1, axis=0)}

        def row_sum(di):
            start = halo + di * GRID_W
            return sum(taps[dj][start:start + tm, :] * w_ref[di + 1, dj + 1] for dj in (-1, 0, 1))

        acc = row_sum(0)
        nrow = lseq // ncol
        if nrow > 1:
            assert ncol == GRID_W
            tpos = (lax.broadcasted_iota(jnp.int32, (tm, 1), 0) + i * tm) & (lseq - 1)
            row = tpos >> int(math.log2(ncol))
            acc = acc + jnp.where(row >= 1, row_sum(-1), 0.0) + jnp.where(row <= nrow - 2, row_sum(1), 0.0)
        o_ref[...] = (acc * jax.nn.sigmoid(acc) * v_ref[...].astype(F32)).astype(o_ref.dtype)

    @pl.when(is_prompt)
    def _():
        conv(geo.lp, geo.lp)

    @pl.when(jnp.logical_not(is_prompt))
    def _():
        conv(geo.ls, GRID_W)


def _conv_gate(geo, u, conv_w):
    nt = u.shape[0]
    f = u.shape[1] // 2
    tm = _pow2_tile(geo.tm, 256)
    assert tm % GRID_W == 0
    tc = _col_tile(f, 512)
    ncb = f // tc
    r = tm // GRID_W
    last = nt // GRID_W - 1
    return pl.pallas_call(
        functools.partial(_conv_gate_kernel, geo, tm),
        grid=(nt // tm, ncb),
        in_specs=[pl.BlockSpec((tm, tc), lambda i, j: (i, j)),
                  pl.BlockSpec((GRID_W, tc), lambda i, j: (jnp.maximum(i * r - 1, 0), j)),
                  pl.BlockSpec((GRID_W, tc), lambda i, j: (jnp.minimum((i + 1) * r, last), j)),
                  pl.BlockSpec((tm, tc), lambda i, j: (i, ncb + j)),
                  pl.BlockSpec((3, 3, 1, tc), lambda i, j: (0, 0, 0, j))],
        out_specs=pl.BlockSpec((tm, tc), lambda i, j: (i, j)),
        out_shape=jax.ShapeDtypeStruct((nt, f), BF16),
        compiler_params=_cparams("parallel", "arbitrary"),
        name="conv_gate",
    )(u, u, u, u, conv_w.reshape(3, 3, 1, f))


def _pad_cols(w, mult):
    n = w.shape[-1]
    pad = _round_up(n, mult) - n
    return jnp.pad(w, [(0, 0)] * (w.ndim - 1) + [(0, pad)]) if pad else w


def kernel(x_prompt, x_sample, state_rwkv, state_gla, c, c_ctx, w_ada, b_ada, norm_mix, norm_ffn, ffn_w_up, ffn_conv, ffn_w_down, norm_final, ev_w_in, ev_w_out, hy_short_w, hy_short_b, hy_w1, hy_b1, hy_w2, hy_b2, hy_w3, hy_freq, hy_bias, rw_mu, rw_w0, rw_w2, rw_a0, rw_a2, rw_g2, rw_kk, rw_ka, rw_rk, rw_ln_w, rw_ln_b, od_w_in, od_w_out, gla_gk2, gla_gk_b, gla_norm):
    bp, lp, d = x_prompt.shape
    bs, ls, _ = x_sample.shape
    geo = _Geo(bp, lp, bs, ls, d)
    depth = w_ada.shape[0]
    x = jnp.concatenate([x_prompt.reshape(geo.np_, d), x_sample.reshape(geo.ns, d)], axis=0)

    cond = jnp.concatenate([c_ctx[None, :], c], axis=0)
    cond = jnp.pad(cond, ((0, SUBLANES - cond.shape[0]), (0, 0)))
    mods_all = _adaln(cond, w_ada, b_ada)

    rw_states, gla_states = [], []
    for layer in range(depth):
        mods = mods_all[layer].reshape(SUBLANES * 6, 1, d)
        if layer % 2 == 0:
            e = layer // 2
            dh = hy_bias.shape[1]
            dr = rw_kk.shape[1]
            perm = _head_minor_perm(dr)
            perm3 = np.concatenate([perm, dr + perm, 2 * dr + perm, np.arange(3 * dr, ev_w_in.shape[2] - 3 * dh)])
            w_hy = ev_w_in[e][:, :3 * dh].astype(BF16)
            w_rw = _pad_cols(ev_w_in[e][:, 3 * dh:][:, perm3], 512).astype(BF16)
            p_hy = _norm_mm(geo, x, norm_mix[layer], mods, 0, w_hy)
            p_rw = _norm_mm(geo, x, norm_mix[layer], mods, 0, w_rw)
            x0, u = _hy_prep(geo, p_hy, hy_short_w[e], hy_short_b[e])
            filt_p = _hyena_filter(lp, hy_w1[e], hy_b1[e], hy_w2[e], hy_b2[e], hy_w3[e], hy_freq[e], dh)
            filt_s = _hyena_filter(ls, hy_w1[e], hy_b1[e], hy_w2[e], hy_b2[e], hy_w3[e], hy_freq[e], dh)
            y_hy = _hyena_long_conv(geo, x0, u, filt_p, filt_s, hy_bias[e])
            mu = _pad_cols(rw_mu[e][perm3], 512)
            y_rw, s_ctx = _rwkv_mixer(geo, p_rw, state_rwkv[:, e], mu, rw_w0[e][:, perm], rw_w2[e][:, :, perm],
                                      rw_a0[e][:, perm], rw_a2[e][:, :, perm], rw_g2[e][:, perm], rw_kk[e][perm],
                                      rw_ka[e][perm], rw_rk[e].reshape(dr)[perm], rw_ln_w[e][perm], rw_ln_b[e][perm])
            rw_states.append(s_ctx)
            w_out = jnp.concatenate([ev_w_out[e][:dh], ev_w_out[e][dh:][perm]], axis=0).astype(BF16)
            x = _res_mm(geo, (y_hy, y_rw), w_out, x, mods, 2)
        else:
            o = layer // 2
            dk = gla_gk2.shape[3]
            dv = gla_norm.shape[1] * GLA_H
            w_main = od_w_in[o][:, :2 * dk + 2 * dv].astype(BF16)
            w_gk = _pad_cols(od_w_in[o][:, 2 * dk + 2 * dv:], LANES).astype(BF16)
            p = _norm_mm(geo, x, norm_mix[layer], mods, 0, w_main)
            gkd = _norm_mm(geo, x, norm_mix[layer], mods, 0, w_gk)
            y, s_ctx = _gla_mixer(geo, p, gkd, state_gla[:, o], gla_gk2[o], gla_gk_b[o], gla_norm[o], dk, dv)
            gla_states.append(s_ctx)
            x = _res_mm(geo, y, od_w_out[o].astype(BF16), x, mods, 2)
        u_ffn = _norm_mm(geo, x, norm_ffn[layer], mods, 3, ffn_w_up[layer].astype(BF16), out_dtype=BF16)
        act = _conv_gate(geo, u_ffn, ffn_conv[layer])
        x = _res_mm(geo, act, ffn_w_down[layer].astype(BF16), x, mods, 5)

    y = _final_norm(geo, x, norm_final)
    y_prompt = y[:geo.np_].reshape(bp, lp, d)
    y_sample = y[geo.np_:].reshape(bs, ls, d)
    return (y_prompt, y_sample, jnp.stack(rw_states, axis=1), jnp.stack(gla_states, axis=1))
```
